```python
import math
import jax
import jax.numpy as jnp
from jax import lax
import numpy as np

D_MODEL = 2048
BATCH = 8
SEQ = 2048
DEPTH = 2

N_A_LAYERS = DEPTH // 2
N_B_LAYERS = DEPTH - N_A_LAYERS

DEEPNORM_ALPHA = (2 * DEPTH) ** 0.25
DEEPNORM_BETA = (8 * DEPTH) ** -0.25
LN_EPS = 1e-5
MACARON_WEIGHT = 0.5

D_FF = ((8 * D_MODEL // 3 + 255) // 256) * 256

RET_HEADS = 8
RET_QK_DIM = D_MODEL // RET_HEADS
RET_V_DIM = 2 * D_MODEL // RET_HEADS
RET_CHUNK = 128
RET_ROPE_BASE = 10000.0

NSA_HEADS = 16
NSA_KV_GROUPS = 4
NSA_HEAD_DIM = D_MODEL // NSA_HEADS
NSA_HEADS_PER_GROUP = NSA_HEADS // NSA_KV_GROUPS
CMP_BLOCK = 32
CMP_STRIDE = 16
CMP_HIDDEN = 4 * NSA_HEAD_DIM
SLC_BLOCK = 64
SLC_TOP = 8
WINDOW = 512
NSA_QUERY_BLOCK = 64
FORCE_BONUS = 1e4

ROPE_THETA = 500000.0
ROPE_DIMS = NSA_HEAD_DIM // 4

NEG_INF = -1e30

kernel_name = "yoco_retention_nsa_macaron_deepnorm"


def layer_norm(x, g, b):
    xf = x.astype(jnp.float32)
    mu = jnp.mean(xf, axis=-1, keepdims=True)
    var = jnp.mean(jnp.square(xf - mu), axis=-1, keepdims=True)
    return ((xf - mu) * lax.rsqrt(var + LN_EPS) * g.astype(jnp.float32) + b.astype(jnp.float32)).astype(x.dtype)


def swiglu(x, w_in, w_out):
    a, u = jnp.split(x @ w_in, 2, axis=-1)
    return (jax.nn.silu(a) * u) @ w_out


def rotary(x, pos, rot_dims, theta):
    half = rot_dims // 2
    inv_freq = jnp.power(jnp.float32(theta), -jnp.arange(half, dtype=jnp.float32) / half)
    ang = pos.astype(jnp.float32)[:, None] * inv_freq[None, :]
    cos, sin = jnp.cos(ang), jnp.sin(ang)
    xf = x.astype(jnp.float32)
    x1, x2 = xf[..., :half], xf[..., half:rot_dims]
    out = jnp.concatenate([x1 * cos - x2 * sin, x2 * cos + x1 * sin, xf[..., rot_dims:]], axis=-1)
    return out.astype(x.dtype)


def retention(x, w_in, gn_g, gn_b, w_out):
    bsz, t_len, _ = x.shape
    h, dk, dv = RET_HEADS, RET_QK_DIM, RET_V_DIM
    proj = x @ w_in
    q, k, v, g = jnp.split(proj, [h * dk, 2 * h * dk, 2 * h * dk + h * dv], axis=-1)
    heads = lambda t, d: t.reshape(bsz, t_len, h, d).transpose(0, 2, 1, 3)
    pos = jnp.arange(t_len)
    q = rotary(heads(q, dk), pos, dk, RET_ROPE_BASE).astype(jnp.float32)
    k = rotary(heads(k, dk), pos, dk, RET_ROPE_BASE).astype(jnp.float32) * (dk ** -0.5)
    v = heads(v, dv).astype(jnp.float32)

    log_gamma = jnp.log(1.0 - jnp.power(2.0, -5.0 - jnp.arange(h, dtype=jnp.float32)))
    c = RET_CHUNK
    n_chunks = t_len // c
    idx = jnp.arange(c, dtype=jnp.float32)
    diff = idx[:, None] - idx[None, :]
    decay_inner = jnp.where(diff >= 0, jnp.exp(log_gamma[:, None, None] * jnp.maximum(diff, 0.0)), 0.0)
    xi = jnp.exp(log_gamma[:, None] * (idx + 1.0))[..., None]
    zeta = jnp.exp(log_gamma[:, None] * (c - 1.0 - idx))[..., None]
    gamma_c = jnp.exp(log_gamma * c)[:, None, None]

    chunk = lambda t: t.reshape(bsz, h, n_chunks, c, t.shape[-1]).transpose(2, 0, 1, 3, 4)

    def step(state, inp):
        qi, ki, vi = inp
        s = jnp.einsum('bhnd,bhmd->bhnm', qi, ki) * decay_inner
        inner = jnp.einsum('bhnm,bhmv->bhnv', s, vi)
        cross = jnp.einsum('bhnd,bhdv->bhnv', qi, state) * xi
        new_state = jnp.einsum('bhmd,bhmv->bhdv', ki * zeta, vi) + gamma_c * state
        return new_state, inner + cross

    state0 = jnp.zeros((bsz, h, dk, dv), jnp.float32)
    _, out = lax.scan(step, state0, (chunk(q), chunk(k), chunk(v)))
    out = out.transpose(1, 0, 3, 2, 4).reshape(bsz, t_len, h, dv)
    mu = jnp.mean(out, axis=-1, keepdims=True)
    var = jnp.mean(jnp.square(out - mu), axis=-1, keepdims=True)
    gn = (out - mu) * lax.rsqrt(var + LN_EPS)
    gn = gn.reshape(bsz, t_len, h * dv) * gn_g.astype(jnp.float32) + gn_b.astype(jnp.float32)
    y = jax.nn.silu(g.astype(jnp.float32)) * gn
    return y.astype(x.dtype) @ w_out


def nsa_shared_kv(h, kv_w, cmp_pos, cmp_w1, cmp_b1, cmp_w2):
    bsz, t_len, _ = h.shape
    g, hd = NSA_KV_GROUPS, NSA_HEAD_DIM
    kv = (h @ kv_w).reshape(bsz, t_len, 6, g, hd).transpose(2, 0, 3, 1, 4)
    n_cmp = (t_len - CMP_BLOCK) // CMP_STRIDE + 1
    blk = jnp.arange(n_cmp)[:, None] * CMP_STRIDE + jnp.arange(CMP_BLOCK)[None, :]
    blocks = kv[0:2][:, :, :, blk, :] + cmp_pos[:, None, None, None]
    flat = blocks.reshape(2, bsz, g, n_cmp, CMP_BLOCK * hd)
    hid = jax.nn.gelu(jnp.einsum('cbgnf,cfe->cbgne', flat, cmp_w1) + cmp_b1[:, None, None, None])
    comp = jnp.einsum('cbgne,ced->cbgnd', hid, cmp_w2)
    pos = jnp.arange(t_len)
    cmp_end = jnp.arange(n_cmp) * CMP_STRIDE + CMP_BLOCK - 1
    k_cmp = rotary(comp[0], cmp_end, ROPE_DIMS, ROPE_THETA)
    v_cmp = comp[1]
    k_slc = rotary(kv[2], pos, ROPE_DIMS, ROPE_THETA)
    k_win = rotary(kv[4], pos, ROPE_DIMS, ROPE_THETA)
    return (k_cmp, v_cmp, k_slc, kv[3], k_win, kv[5])


def overlap_matrix(t_len):
    n_cmp = (t_len - CMP_BLOCK) // CMP_STRIDE + 1
    n_slc = t_len // SLC_BLOCK
    cs = np.arange(n_cmp)[:, None] * CMP_STRIDE
    ss = np.arange(n_slc)[None, :] * SLC_BLOCK
    return jnp.asarray(((cs < ss + SLC_BLOCK) & (cs + CMP_BLOCK > ss)).astype(np.float32))


def nsa_attention(x, kvs, w_in, w_out):
    k_cmp, v_cmp, k_slc, v_slc, k_win, v_win = kvs
    bsz, t_len, _ = x.shape
    h, g, hg, hd = NSA_HEADS, NSA_KV_GROUPS, NSA_HEADS_PER_GROUP, NSA_HEAD_DIM
    proj = x @ w_in
    q = proj[..., :h * hd].reshape(bsz, t_len, g, hg, hd).transpose(0, 2, 3, 1, 4)
    gates = jax.nn.sigmoid(proj[..., h * hd:].astype(jnp.float32)).reshape(bsz, t_len, g, hg, 3).transpose(0, 2, 3, 1, 4)
    pos = jnp.arange(t_len)
    q = rotary(q, pos, ROPE_DIMS, ROPE_THETA) * (hd ** -0.5)

    n_cmp = k_cmp.shape[2]
    cmp_end = jnp.arange(n_cmp) * CMP_STRIDE + CMP_BLOCK - 1
    cmp_mask = cmp_end[None, :] <= pos[:, None]
    has_cmp = jnp.any(cmp_mask, axis=-1).astype(jnp.float32)[:, None]
    s_cmp = jnp.einsum('bgitd,bgnd->bgitn', q, k_cmp).astype(jnp.float32)
    p_cmp = jax.nn.softmax(jnp.where(cmp_mask, s_cmp, NEG_INF), axis=-1) * has_cmp
    o_cmp = jnp.einsum('bgitn,bgnd->bgitd', p_cmp.astype(v_cmp.dtype), v_cmp)

    n_slc = t_len // SLC_BLOCK
    top = min(SLC_TOP, n_slc)
    p_slc = jnp.einsum('bgitn,nj->bgtj', p_cmp, overlap_matrix(t_len))
    blk_ids = jnp.arange(n_slc)
    cur = pos // SLC_BLOCK
    valid = (blk_ids * SLC_BLOCK)[None, :] <= pos[:, None]
    forced = (blk_ids[None, :] == 0) | (blk_ids[None, :] == cur[:, None]) | (blk_ids[None, :] == cur[:, None] - 1)
    score = jnp.where(valid, p_slc + jnp.where(forced, FORCE_BONUS, 0.0), -1.0)
    _, sel_idx = lax.top_k(score, top)

    k_slc_blk = k_slc.reshape(bsz, g, n_slc, SLC_BLOCK, hd)
    v_slc_blk = v_slc.reshape(bsz, g, n_slc, SLC_BLOCK, hd)
    pad = ((0, 0), (0, 0), (WINDOW, 0), (0, 0))
    k_win_pad = jnp.pad(k_win, pad)
    v_win_pad = jnp.pad(v_win, pad)
    qb_len = NSA_QUERY_BLOCK
    n_qb = t_len // qb_len
    gather = jax.vmap(jax.vmap(lambda kb, ib: kb[ib]))

    def block_fn(qb):
        s0 = qb * qb_len
        q_b = lax.dynamic_slice_in_dim(q, s0, qb_len, axis=3)
        t_b = s0 + jnp.arange(qb_len)
        idx_b = lax.dynamic_slice_in_dim(sel_idx, s0, qb_len, axis=2)
        kg = gather(k_slc_blk, idx_b).reshape(bsz, g, qb_len, top * SLC_BLOCK, hd)
        vg = gather(v_slc_blk, idx_b).reshape(bsz, g, qb_len, top * SLC_BLOCK, hd)
        tok = (idx_b[..., None] * SLC_BLOCK + jnp.arange(SLC_BLOCK)).reshape(bsz, g, qb_len, top * SLC_BLOCK)
        m_slc = tok <= t_b[:, None]
        s = jnp.einsum('bgiqd,bgqkd->bgiqk', q_b, kg).astype(jnp.float32)
        p = jax.nn.softmax(jnp.where(m_slc[:, :, None], s, NEG_INF), axis=-1)
        o_s = jnp.einsum('bgiqk,bgqkd->bgiqd', p.astype(vg.dtype), vg)
        kw = lax.dynamic_slice_in_dim(k_win_pad, s0, WINDOW + qb_len, axis=2)
        vw = lax.dynamic_slice_in_dim(v_win_pad, s0, WINDOW + qb_len, axis=2)
        kpos = s0 - WINDOW + jnp.arange(WINDOW + qb_len)
        m_win = (kpos[None, :] >= 0) & (kpos[None, :] <= t_b[:, None]) & (kpos[None, :] > t_b[:, None] - WINDOW)
        sw = jnp.einsum('bgiqd,bgkd->bgiqk', q_b, kw).astype(jnp.float32)
        pw = jax.nn.softmax(jnp.where(m_win, sw, NEG_INF), axis=-1)
        o_w = jnp.einsum('bgiqk,bgkd->bgiqd', pw.astype(vw.dtype), vw)
        return o_s, o_w

    o_slc, o_win = lax.map(block_fn, jnp.arange(n_qb))
    o_slc = jnp.moveaxis(o_slc, 0, 3).reshape(bsz, g, hg, t_len, hd)
    o_win = jnp.moveaxis(o_win, 0, 3).reshape(bsz, g, hg, t_len, hd)

    o = gates[..., 0:1] * o_cmp + gates[..., 1:2] * o_slc + gates[..., 2:3] * o_win
    o = o.astype(x.dtype).transpose(0, 3, 1, 2, 4).reshape(bsz, t_len, h * hd)
    return o @ w_out


def setup_inputs(seed: int = 0) -> dict:
    key = jax.random.key(seed)
    ks = jax.random.split(key, 24)
    f32 = jnp.float32

    def w(k, shape, fan_in, scale=1.0):
        return jax.random.normal(k, shape, f32) * (scale * fan_in ** -0.5)

    ret_cols = 2 * RET_HEADS * RET_QK_DIM + 2 * RET_HEADS * RET_V_DIM
    nsa_cols = NSA_HEADS * NSA_HEAD_DIM + 3 * NSA_HEADS
    return {
        "x": jax.random.normal(ks[0], (BATCH, SEQ, D_MODEL), f32),
        "ln_g": 1.0 + 0.02 * jax.random.normal(ks[1], (DEPTH, 3, D_MODEL), f32),
        "ln_b": 0.02 * jax.random.normal(ks[2], (DEPTH, 3, D_MODEL), f32),
        "ffn1_w_in": w(ks[3], (DEPTH, D_MODEL, 2 * D_FF), D_MODEL),
        "ffn1_w_out": w(ks[4], (DEPTH, D_FF, D_MODEL), D_FF, DEEPNORM_BETA),
        "ffn2_w_in": w(ks[5], (DEPTH, D_MODEL, 2 * D_FF), D_MODEL),
        "ffn2_w_out": w(ks[6], (DEPTH, D_FF, D_MODEL), D_FF, DEEPNORM_BETA),
        "ret_w_in": w(ks[7], (N_A_LAYERS, D_MODEL, ret_cols), D_MODEL),
        "ret_gn_g": 1.0 + 0.02 * jax.random.normal(ks[8], (N_A_LAYERS, RET_HEADS * RET_V_DIM), f32),
        "ret_gn_b": 0.02 * jax.random.normal(ks[9], (N_A_LAYERS, RET_HEADS * RET_V_DIM), f32),
        "ret_w_out": w(ks[10], (N_A_LAYERS, RET_HEADS * RET_V_DIM, D_MODEL), RET_HEADS * RET_V_DIM, DEEPNORM_BETA),
        "kv_w": w(ks[11], (D_MODEL, 6 * NSA_KV_GROUPS * NSA_HEAD_DIM), D_MODEL),
        "cmp_pos": 0.1 * jax.random.normal(ks[12], (2, CMP_BLOCK, NSA_HEAD_DIM), f32),
        "cmp_w1": w(ks[13], (2, CMP_BLOCK * NSA_HEAD_DIM, CMP_HIDDEN), CMP_BLOCK * NSA_HEAD_DIM),
        "cmp_b1": 0.02 * jax.random.normal(ks[14], (2, CMP_HIDDEN), f32),
        "cmp_w2": w(ks[15], (2, CMP_HIDDEN, NSA_HEAD_DIM), CMP_HIDDEN),
        "nsa_w_q": w(ks[16], (N_B_LAYERS, D_MODEL, nsa_cols), D_MODEL),
        "nsa_w_out": w(ks[17], (N_B_LAYERS, NSA_HEADS * NSA_HEAD_DIM, D_MODEL), NSA_HEADS * NSA_HEAD_DIM, DEEPNORM_BETA),
    }


def reference(x, ln_g, ln_b, ffn1_w_in, ffn1_w_out, ffn2_w_in, ffn2_w_out, ret_w_in, ret_gn_g, ret_gn_b, ret_w_out, kv_w, cmp_pos, cmp_w1, cmp_b1, cmp_w2, nsa_w_q, nsa_w_out):
    kvs = None
    for layer in range(DEPTH):
        x = layer_norm(DEEPNORM_ALPHA * x + MACARON_WEIGHT * swiglu(x, ffn1_w_in[layer], ffn1_w_out[layer]), ln_g[layer, 0], ln_b[layer, 0])
        if layer < N_A_LAYERS:
            mix = retention(x, ret_w_in[layer], ret_gn_g[layer], ret_gn_b[layer], ret_w_out[layer])
        else:
            j = layer - N_A_LAYERS
            mix = nsa_attention(x, kvs, nsa_w_q[j], nsa_w_out[j])
        x = layer_norm(DEEPNORM_ALPHA * x + mix, ln_g[layer, 1], ln_b[layer, 1])
        x = layer_norm(DEEPNORM_ALPHA * x + MACARON_WEIGHT * swiglu(x, ffn2_w_in[layer], ffn2_w_out[layer]), ln_g[layer, 2], ln_b[layer, 2])
        if layer == N_A_LAYERS - 1:
            kvs = nsa_shared_kv(x, kv_w, cmp_pos, cmp_w1, cmp_b1, cmp_w2)
    return x
```

```python
import functools

import numpy as np
import jax
import jax.numpy as jnp
from jax import lax
from jax.experimental import pallas as pl
from jax.experimental.pallas import tpu as pltpu

F32 = jnp.float32
BF16 = jnp.bfloat16

DEPTH = 2
N_A_LAYERS = DEPTH // 2
DEEPNORM_ALPHA = (2 * DEPTH) ** 0.25
LN_EPS = 1e-5
MACARON_WEIGHT = 0.5

RET_HEADS = 8
RET_CHUNK = 128
RET_ROPE_BASE = 10000.0

NSA_HEADS = 16
NSA_KV_GROUPS = 4
NSA_HEADS_PER_GROUP = NSA_HEADS // NSA_KV_GROUPS
CMP_BLOCK = 32
CMP_STRIDE = 16
SLC_BLOCK = 64
SLC_TOP = 8
WINDOW = 512
FORCE_BONUS = 1e4
ROPE_THETA = 500000.0
NEG_INF = -1e30

V7X_VMEM_BYTES = 64 * 1024 * 1024
VMEM_LIMIT = V7X_VMEM_BYTES - 8 * 1024 * 1024
LANES = 128

NSA_Q_TILE = 128
NSA_KEY_CHUNK = 512


def _params(*semantics):
    return pltpu.CompilerParams(dimension_semantics=semantics, vmem_limit_bytes=VMEM_LIMIT)


def _layer_norm(y, g, b):
    mu = jnp.mean(y, axis=-1, keepdims=True)
    d = y - mu
    var = jnp.mean(d * d, axis=-1, keepdims=True)
    return d * lax.rsqrt(var + LN_EPS) * g + b


def _dot(a, b):
    return jnp.dot(a, b, preferred_element_type=F32)


def _dot_nt(a, b):
    return lax.dot_general(a, b, (((1,), (1,)), ((), ())), preferred_element_type=F32)


def _dot_tn(a, b):
    return lax.dot_general(a, b, (((0,), (0,)), ((), ())), preferred_element_type=F32)


def _ffn_kernel(x_ref, wa_ref, wu_ref, wo_ref, g_ref, b_ref, o_ref, acc_ref, xb_ref, *, nf):
    f = pl.program_id(1)

    @pl.when(f == 0)
    def _():
        acc_ref[...] = jnp.zeros_like(acc_ref)
        xb_ref[...] = x_ref[...].astype(BF16)

    xb = xb_ref[...]
    a = _dot(xb, wa_ref[...])
    u = _dot(xb, wu_ref[...])
    h = (a * jax.nn.sigmoid(a)) * u
    acc_ref[...] += _dot(h.astype(BF16), wo_ref[...])

    @pl.when(f == nf - 1)
    def _():
        y = DEEPNORM_ALPHA * x_ref[...] + MACARON_WEIGHT * acc_ref[...]
        o_ref[...] = _layer_norm(y, g_ref[...], b_ref[...])


def _ffn(x, w_in, w_out, g, b):
    m, d = x.shape
    f_dim = w_out.shape[0]
    tm, tf = 512, 512
    nf = f_dim // tf
    return pl.pallas_call(
        functools.partial(_ffn_kernel, nf=nf),
        grid=(m // tm, nf),
        in_specs=[
            pl.BlockSpec((tm, d), lambda i, f: (i, 0)),
            pl.BlockSpec((d, tf), lambda i, f: (0, f)),
            pl.BlockSpec((d, tf), lambda i, f: (0, f + nf)),
            pl.BlockSpec((tf, d), lambda i, f: (f, 0)),
            pl.BlockSpec((1, d), lambda i, f: (0, 0)),
            pl.BlockSpec((1, d), lambda i, f: (0, 0)),
        ],
        out_specs=pl.BlockSpec((tm, d), lambda i, f: (i, 0)),
        out_shape=jax.ShapeDtypeStruct((m, d), F32),
        scratch_shapes=[pltpu.VMEM((tm, d), F32), pltpu.VMEM((tm, d), BF16)],
        compiler_params=_params("parallel", "arbitrary"),
        name="ffn_deepnorm",
    )(x, w_in, w_in, w_out, g.reshape(1, d), b.reshape(1, d))


def _proj_kernel(x_ref, w_ref, o_ref, xb_ref, *, act):
    @pl.when(pl.program_id(1) == 0)
    def _():
        xb_ref[...] = x_ref[...].astype(BF16)

    acc = _dot(xb_ref[...], w_ref[...])
    if act == "sigmoid":
        acc = jax.nn.sigmoid(acc)
    o_ref[...] = acc.astype(o_ref.dtype)


def _proj(x, w, col0, ncols, out_dtype, act=None, name="proj"):
    m, k = x.shape
    tm, tn = 1024, 512
    assert col0 % tn == 0 and ncols % tn == 0
    return pl.pallas_call(
        functools.partial(_proj_kernel, act=act),
        grid=(m // tm, ncols // tn),
        in_specs=[
            pl.BlockSpec((tm, k), lambda i, j: (i, 0)),
            pl.BlockSpec((k, tn), lambda i, j: (0, j + col0 // tn)),
        ],
        out_specs=pl.BlockSpec((tm, tn), lambda i, j: (i, j)),
        out_shape=jax.ShapeDtypeStruct((m, ncols), out_dtype),
        scratch_shapes=[pltpu.VMEM((tm, k), BF16)],
        compiler_params=_params("parallel", "arbitrary"),
        name=name,
    )(x, w)


def _ret_qkv_kernel(x_ref, w_ref, cos_ref, sin_ref, o_ref, xb_ref, *, n_rot, k_start, k_scale, half):
    j = pl.program_id(1)

    @pl.when(j == 0)
    def _():
        xb_ref[...] = x_ref[...].astype(BF16)

    acc = _dot(xb_ref[...], w_ref[...])
    tn = acc.shape[1]

    @pl.when(j < n_rot)
    def _():
        scale = jnp.where(j >= k_start, k_scale, 1.0).astype(F32)
        cos = cos_ref[...]
        sin = sin_ref[...]
        for h in range(tn // (2 * half)):
            c0 = h * 2 * half
            x1 = acc[:, c0:c0 + half]
            x2 = acc[:, c0 + half:c0 + 2 * half]
            o_ref[:, c0:c0 + half] = ((x1 * cos - x2 * sin) * scale).astype(o_ref.dtype)
            o_ref[:, c0 + half:c0 + 2 * half] = ((x2 * cos + x1 * sin) * scale).astype(o_ref.dtype)

    @pl.when(j >= n_rot)
    def _():
        o_ref[...] = acc.astype(o_ref.dtype)


def _ret_qkv(x, w, cos, sin, t_len, dk, ncols):
    m, k = x.shape
    tm, tn = 1024, 512
    qk_tiles = RET_HEADS * dk // tn
    kern = functools.partial(_ret_qkv_kernel, n_rot=2 * qk_tiles, k_start=qk_tiles,
                             k_scale=dk ** -0.5, half=dk // 2)
    tpb = t_len // tm
    return pl.pallas_call(
        kern,
        grid=(m // tm, ncols // tn),
        in_specs=[
            pl.BlockSpec((tm, k), lambda i, j: (i, 0)),
            pl.BlockSpec((k, tn), lambda i, j: (0, j)),
            pl.BlockSpec((tm, dk // 2), lambda i, j: (i % tpb, 0)),
            pl.BlockSpec((tm, dk // 2), lambda i, j: (i % tpb, 0)),
        ],
        out_specs=pl.BlockSpec((tm, tn), lambda i, j: (i, j)),
        out_shape=jax.ShapeDtypeStruct((m, ncols), BF16),
        scratch_shapes=[pltpu.VMEM((tm, k), BF16)],
        compiler_params=_params("parallel", "arbitrary"),
        name="ret_qkv_proj",
    )(x, w, cos, sin)


def _partial_rotary(x, c, s, half):
    lane = lax.broadcasted_iota(jnp.int32, x.shape, 1)
    partner = jnp.where(lane < half, pltpu.roll(x, LANES - half, 1), pltpu.roll(x, half, 1))
    return x * c + partner * s


def _head_proj_kernel(x_ref, w_ref, c_ref, s_ref, o_ref, xb_ref, *, hd, half, scale, rotate, split):
    j = pl.program_id(1)

    @pl.when(j == 0)
    def _():
        xb_ref[...] = x_ref[...].astype(BF16)

    acc = _dot(xb_ref[...], w_ref[...])
    if rotate == "even":
        rot = (j % 2) == 0
        c = jnp.where(rot, c_ref[...], 1.0)
        s = jnp.where(rot, s_ref[...], 0.0)
    elif rotate == "all":
        c, s = c_ref[...], s_ref[...]
    for h in range(acc.shape[1] // hd):
        piece = acc[:, h * hd:(h + 1) * hd]
        if rotate != "none":
            piece = _partial_rotary(piece, c, s, half)
        if scale != 1.0:
            piece = piece * scale
        if split:
            o_ref[h] = piece.astype(o_ref.dtype)
        else:
            o_ref[:, h * hd:(h + 1) * hd] = piece.astype(o_ref.dtype)


def _head_proj(x, w, c_tab, s_tab, t_len, *, col0, nslots, out_dtype, half, scale, rotate, split, name):
    m, k = x.shape
    hd = LANES
    tm, tn = 1024, 512
    hpt = tn // hd
    tpb = t_len // tm
    kern = functools.partial(_head_proj_kernel, hd=hd, half=half, scale=scale, rotate=rotate, split=split)
    if split:
        out_spec = pl.BlockSpec((None, hpt, tm, hd), lambda i, j: (j, 0, i, 0))
        out_shape = jax.ShapeDtypeStruct((nslots, hpt, m, hd), out_dtype)
    else:
        out_spec = pl.BlockSpec((tm, tn), lambda i, j: (i, j))
        out_shape = jax.ShapeDtypeStruct((m, nslots * tn), out_dtype)
    return pl.pallas_call(
        kern,
        grid=(m // tm, nslots),
        in_specs=[
            pl.BlockSpec((tm, k), lambda i, j: (i, 0)),
            pl.BlockSpec((k, tn), lambda i, j: (0, j + col0 // tn)),
            pl.BlockSpec((tm, hd), lambda i, j: (i % tpb, 0)),
            pl.BlockSpec((tm, hd), lambda i, j: (i % tpb, 0)),
        ],
        out_specs=out_spec,
        out_shape=out_shape,
        scratch_shapes=[pltpu.VMEM((tm, k), BF16)],
        compiler_params=_params("parallel", "arbitrary"),
        name=name,
    )(x, w, c_tab, s_tab)


def _out_proj_kernel(y_ref, w_ref, x_ref, g_ref, b_ref, o_ref, acc_ref, *, nk):
    kk = pl.program_id(1)

    @pl.when(kk == 0)
    def _():
        acc_ref[...] = jnp.zeros_like(acc_ref)

    acc_ref[...] += _dot(y_ref[...], w_ref[...])

    @pl.when(kk == nk - 1)
    def _():
        o_ref[...] = _layer_norm(DEEPNORM_ALPHA * x_ref[...] + acc_ref[...], g_ref[...], b_ref[...])


def _out_proj(y, w, x, g, b, name):
    m, k = y.shape
    d = w.shape[1]
    tm, tk = 512, 1024
    nk = k // tk
    return pl.pallas_call(
        functools.partial(_out_proj_kernel, nk=nk),
        grid=(m // tm, nk),
        in_specs=[
            pl.BlockSpec((tm, tk), lambda i, kk: (i, kk)),
            pl.BlockSpec((tk, d), lambda i, kk: (kk, 0)),
            pl.BlockSpec((tm, d), lambda i, kk: (i, 0)),
            pl.BlockSpec((1, d), lambda i, kk: (0, 0)),
            pl.BlockSpec((1, d), lambda i, kk: (0, 0)),
        ],
        out_specs=pl.BlockSpec((tm, d), lambda i, kk: (i, 0)),
        out_shape=jax.ShapeDtypeStruct((m, d), F32),
        scratch_shapes=[pltpu.VMEM((tm, d), F32)],
        compiler_params=_params("parallel", "arbitrary"),
        name=name,
    )(y, w, x, g.reshape(1, d), b.reshape(1, d))


def _retention_kernel(gam_ref, q_ref, k_ref, v_ref, g_ref, dec_ref, xi_ref, zeta_ref, gng_ref, gnb_ref,
                      o_ref, state_ref):
    h = pl.program_id(1)

    @pl.when(pl.program_id(2) == 0)
    def _():
        state_ref[...] = jnp.zeros_like(state_ref)

    q = q_ref[...]
    k = k_ref[...]
    v = v_ref[...]
    state = state_ref[...]
    s = _dot_nt(q, k) * dec_ref[...]
    inner = _dot(s.astype(BF16), v)
    cross = _dot(q, state.astype(BF16)) * xi_ref[...]
    out = inner + cross
    kz = (k.astype(F32) * zeta_ref[...]).astype(BF16)
    state_ref[...] = _dot_tn(kz, v) + gam_ref[h] * state

    mu = jnp.mean(out, axis=-1, keepdims=True)
    d = out - mu
    var = jnp.mean(d * d, axis=-1, keepdims=True)
    gn = d * lax.rsqrt(var + LN_EPS) * gng_ref[...] + gnb_ref[...]
    gate = g_ref[...]
    o_ref[...] = ((gate * jax.nn.sigmoid(gate)) * gn).astype(o_ref.dtype)


def _retention(qkv, gate, gn_g, gn_b, bsz, t_len, dk, dv):
    h = RET_HEADS
    c = RET_CHUNK
    nc = t_len // c
    m = bsz * t_len
    log_gamma = jnp.log(1.0 - jnp.power(2.0, -5.0 - jnp.arange(h, dtype=F32)))
    idx = jnp.arange(c, dtype=F32)
    diff = idx[:, None] - idx[None, :]
    decay = jnp.where(diff >= 0, jnp.exp(log_gamma[:, None, None] * jnp.maximum(diff, 0.0)), 0.0)
    xi = jnp.exp(log_gamma[:, None] * (idx + 1.0))[..., None]
    zeta = jnp.exp(log_gamma[:, None] * (c - 1.0 - idx))[..., None]
    gamma_c = jnp.exp(log_gamma * c)
    k_blk0 = h * dk // dk
    v_blk0 = 2 * h * dk // dv
    row = lambda b, hh, cc: b * nc + cc
    return pl.pallas_call(
        _retention_kernel,
        grid=(bsz, h, nc),
        in_specs=[
            pl.BlockSpec(memory_space=pltpu.SMEM),
            pl.BlockSpec((c, dk), lambda b, hh, cc: (row(b, hh, cc), hh)),
            pl.BlockSpec((c, dk), lambda b, hh, cc: (row(b, hh, cc), k_blk0 + hh)),
            pl.BlockSpec((c, dv), lambda b, hh, cc: (row(b, hh, cc), v_blk0 + hh)),
            pl.BlockSpec((c, dv), lambda b, hh, cc: (row(b, hh, cc), hh)),
            pl.BlockSpec((None, c, c), lambda b, hh, cc: (hh, 0, 0)),
            pl.BlockSpec((None, c, 1), lambda b, hh, cc: (hh, 0, 0)),
            pl.BlockSpec((None, c, 1), lambda b, hh, cc: (hh, 0, 0)),
            pl.BlockSpec((1, dv), lambda b, hh, cc: (0, hh)),
            pl.BlockSpec((1, dv), lambda b, hh, cc: (0, hh)),
        ],
        out_specs=pl.BlockSpec((c, dv), lambda b, hh, cc: (row(b, hh, cc), hh)),
        out_shape=jax.ShapeDtypeStruct((m, h * dv), BF16),
        scratch_shapes=[pltpu.VMEM((dk, dv), F32)],
        compiler_params=_params("parallel", "parallel", "arbitrary"),
        name="retention_chunks",
    )(gamma_c, qkv, qkv, qkv, gate, decay, xi, zeta, gn_g.reshape(1, h * dv), gn_b.reshape(1, h * dv))


def _cmp_mlp_kernel(x_ref, pos_ref, w1a_ref, w1b_ref, b1_ref, w2_ref, c_ref, s_ref, o_ref, *, half):
    x = x_ref[...]
    pos = pos_ref[...]
    a = _dot((x + pos[0:1, :]).astype(BF16), w1a_ref[...])
    bm = _dot((x + pos[1:2, :]).astype(BF16), w1b_ref[...])
    nrows = x.shape[0]
    hid = jax.nn.gelu(a + pltpu.roll(bm, nrows - 1, 0) + b1_ref[...])
    comp = _dot(hid.astype(BF16), w2_ref[...])
    is_key = pl.program_id(0) == 0
    c = jnp.where(is_key, c_ref[...], 1.0)
    s = jnp.where(is_key, s_ref[...], 0.0)
    o_ref[...] = _partial_rotary(comp, c, s, half).astype(o_ref.dtype)


def _cmp_mlp(raw, cmp_pos, w1, b1, w2, c_tab, s_tab, bsz, t_len, half):
    g, hd = NSA_KV_GROUPS, LANES
    nrows = t_len // CMP_STRIDE
    feat = CMP_STRIDE * hd
    hidden = w1.shape[-1]
    x = raw.reshape(2, g, bsz, nrows, feat)
    pos = cmp_pos.reshape(2, CMP_BLOCK // CMP_STRIDE, feat)
    return pl.pallas_call(
        functools.partial(_cmp_mlp_kernel, half=half),
        grid=(2, g, bsz),
        in_specs=[
            pl.BlockSpec((None, None, None, nrows, feat), lambda c, gg, b: (c, gg, b, 0, 0)),
            pl.BlockSpec((None, 2, feat), lambda c, gg, b: (c, 0, 0)),
            pl.BlockSpec((None, feat, hidden), lambda c, gg, b: (c, 0, 0)),
            pl.BlockSpec((None, feat, hidden), lambda c, gg, b: (c, 1, 0)),
            pl.BlockSpec((None, 1, hidden), lambda c, gg, b: (c, 0, 0)),
            pl.BlockSpec((None, hidden, hd), lambda c, gg, b: (c, 0, 0)),
            pl.BlockSpec((nrows, hd), lambda c, gg, b: (0, 0)),
            pl.BlockSpec((nrows, hd), lambda c, gg, b: (0, 0)),
        ],
        out_specs=pl.BlockSpec((None, None, None, nrows, hd), lambda c, gg, b: (c, gg, b, 0, 0)),
        out_shape=jax.ShapeDtypeStruct((2, g, bsz, nrows, hd), BF16),
        compiler_params=_params("arbitrary", "arbitrary", "arbitrary"),
        name="cmp_mlp",
    )(x, pos, w1, w1, b1.reshape(2, 1, hidden), w2, c_tab, s_tab)


def _softmax_last(s):
    m = jnp.max(s, axis=-1, keepdims=True)
    e = jnp.exp(s - m)
    return e / jnp.sum(e, axis=-1, keepdims=True)


def _nsa_kernel(q_ref, gate_ref, kc_ref, vc_ref, ks_ref, vs_ref, kw_ref, vw_ref, ov_ref, ex_ref,
                o_ref, mask_ref, *, tq, tc, hg, hd):
    qi = pl.program_id(2)
    t0 = qi * tq
    rows = hg * tq
    q = q_ref[...]
    q4 = jnp.concatenate([q[:, i * hd:(i + 1) * hd] for i in range(hg)], axis=0)
    tpos = t0 + lax.broadcasted_iota(jnp.int32, (tq, 1), 0)

    ncmp = kc_ref.shape[0]
    s3 = _dot_nt(q4, kc_ref[...]).reshape(hg, tq, ncmp)
    cmp_end = lax.broadcasted_iota(jnp.int32, (tq, ncmp), 1) * CMP_STRIDE + (CMP_BLOCK - 1)
    s3 = jnp.where((cmp_end <= tpos)[None], s3, NEG_INF)
    has_cmp = (tpos >= CMP_BLOCK - 1).astype(F32)
    p3 = (_softmax_last(s3) * has_cmp[None]).astype(BF16)
    o_cmp = _dot(p3.reshape(rows, ncmp), vc_ref[...])

    nslc = ov_ref.shape[1]
    p_slc = _dot(jnp.concatenate([p3[i] for i in range(hg)], axis=1), ov_ref[...])
    blk = lax.broadcasted_iota(jnp.int32, (tq, nslc), 1)
    cur = jnp.right_shift(tpos, SLC_BLOCK.bit_length() - 1)
    valid = blk * SLC_BLOCK <= tpos
    forced = (blk == 0) | (blk == cur) | (blk == cur - 1)
    score = jnp.where(valid, p_slc + jnp.where(forced, FORCE_BONUS, 0.0), -1.0)
    rank = jnp.zeros((tq, nslc), jnp.int32)
    for kb in range(nslc):
        ck = score[:, kb:kb + 1]
        rank += ((ck > score) | ((ck == score) & (blk > kb))).astype(jnp.int32)
    sel = jnp.where(rank < SLC_TOP, 1.0, 0.0).astype(BF16)
    mask_ref[...] = _dot(sel, ex_ref[...])

    def chunk(c, carry):
        m_i, l_i, acc = carry
        k0 = pl.multiple_of(c * tc, tc)
        s = _dot_nt(q4, ks_ref[pl.ds(k0, tc), :]).reshape(hg, tq, tc)
        kpos = k0 + lax.broadcasted_iota(jnp.int32, (tq, tc), 1)
        ok = (mask_ref[:, pl.ds(k0, tc)] > 0.5) & (kpos <= tpos)
        s = jnp.where(ok[None], s, NEG_INF)
        m_new = jnp.maximum(m_i, jnp.max(s, axis=-1, keepdims=True))
        alpha = jnp.exp(m_i - m_new)
        p = jnp.exp(s - m_new)
        l_new = alpha * l_i + jnp.sum(p, axis=-1, keepdims=True)
        pv = _dot(p.astype(BF16).reshape(rows, tc), vs_ref[pl.ds(k0, tc), :])
        return m_new, l_new, acc * alpha.reshape(rows, 1) + pv

    init = (jnp.full((hg, tq, 1), NEG_INF, F32), jnp.zeros((hg, tq, 1), F32), jnp.zeros((rows, hd), F32))
    n_chunks = lax.div(t0, tc) + 1
    _, l_f, acc_f = lax.fori_loop(0, n_chunks, chunk, init)
    o_slc = acc_f / l_f.reshape(rows, 1)

    wlen = WINDOW + tq
    w0 = pl.multiple_of(jnp.maximum(qi - WINDOW // tq, 0) * tq, tq)
    sw = _dot_nt(q4, kw_ref[pl.ds(w0, wlen), :]).reshape(hg, tq, wlen)
    kpos = w0 + lax.broadcasted_iota(jnp.int32, (tq, wlen), 1)
    okw = (kpos <= tpos) & (kpos > tpos - WINDOW)
    pw = _softmax_last(jnp.where(okw[None], sw, NEG_INF)).astype(BF16)
    o_win = _dot(pw.reshape(rows, wlen), vw_ref[pl.ds(w0, wlen), :])

    gts = gate_ref[...]
    for i in range(hg):
        r = slice(i * tq, (i + 1) * tq)
        o = (gts[:, 3 * i:3 * i + 1] * o_cmp[r] + gts[:, 3 * i + 1:3 * i + 2] * o_slc[r]
             + gts[:, 3 * i + 2:3 * i + 3] * o_win[r])
        o_ref[:, i * hd:(i + 1) * hd] = o.astype(o_ref.dtype)


def _nsa_attention(q, gates, kv_cmp, kv_rest, bsz, t_len):
    g, hg, hd = NSA_KV_GROUPS, NSA_HEADS_PER_GROUP, LANES
    tq, tc = NSA_Q_TILE, NSA_KEY_CHUNK
    nq = t_len // tq
    ncmp_pad = t_len // CMP_STRIDE
    n_cmp = (t_len - CMP_BLOCK) // CMP_STRIDE + 1
    n_slc = t_len // SLC_BLOCK
    cs = np.arange(ncmp_pad)[:, None] * CMP_STRIDE
    ss = np.arange(n_slc)[None, :] * SLC_BLOCK
    ov = ((cs < ss + SLC_BLOCK) & (cs + CMP_BLOCK > ss) & (np.arange(ncmp_pad)[:, None] < n_cmp)).astype(np.float32)
    ov = jnp.asarray(np.tile(ov, (hg, 1)), BF16)
    expand = jnp.asarray((np.arange(t_len)[None, :] // SLC_BLOCK == np.arange(n_slc)[:, None]).astype(np.float32), BF16)
    kv_spec = lambda slot: pl.BlockSpec((None, None, t_len, hd), lambda b, gg, qi: (slot, gg, b, 0))
    cmp_spec = lambda slot: pl.BlockSpec((None, None, None, ncmp_pad, hd), lambda b, gg, qi: (slot, gg, b, 0, 0))
    return pl.pallas_call(
        functools.partial(_nsa_kernel, tq=tq, tc=tc, hg=hg, hd=hd),
        grid=(bsz, g, nq),
        in_specs=[
            pl.BlockSpec((tq, hg * hd), lambda b, gg, qi: (b * nq + qi, gg)),
            pl.BlockSpec((tq, LANES), lambda b, gg, qi: (b * nq + qi, gg)),
            cmp_spec(0), cmp_spec(1),
            kv_spec(0), kv_spec(1), kv_spec(2), kv_spec(3),
            pl.BlockSpec((hg * ncmp_pad, n_slc), lambda b, gg, qi: (0, 0)),
            pl.BlockSpec((n_slc, t_len), lambda b, gg, qi: (0, 0)),
        ],
        out_specs=pl.BlockSpec((tq, hg * hd), lambda b, gg, qi: (b * nq + qi, gg)),
        out_shape=jax.ShapeDtypeStruct((bsz * t_len, g * hg * hd), BF16),
        scratch_shapes=[pltpu.VMEM((tq, t_len), F32)],
        compiler_params=_params("parallel", "parallel", "arbitrary"),
        name="nsa_attention",
    )(q, gates, kv_cmp, kv_cmp, kv_rest, kv_rest, kv_rest, kv_rest, ov, expand)


def _rope_tables(pos, rot_dims, theta):
    half = rot_dims // 2
    inv_freq = jnp.power(jnp.float32(theta), -jnp.arange(half, dtype=F32) / half)
    ang = pos.astype(F32)[:, None] * inv_freq[None, :]
    cos, sin = jnp.cos(ang), jnp.sin(ang)
    n = pos.shape[0]
    pad = LANES - rot_dims
    c_tab = jnp.concatenate([cos, cos, jnp.ones((n, pad), F32)], axis=1)
    s_tab = jnp.concatenate([-sin, sin, jnp.zeros((n, pad), F32)], axis=1)
    return c_tab, s_tab


def kernel(x, ln_g, ln_b, ffn1_w_in, ffn1_w_out, ffn2_w_in, ffn2_w_out, ret_w_in, ret_gn_g, ret_gn_b, ret_w_out, kv_w, cmp_pos, cmp_w1, cmp_b1, cmp_w2, nsa_w_q, nsa_w_out):
    bsz, t_len, d = x.shape
    m = bsz * t_len
    bf = lambda w: w.astype(BF16)
    h = x.reshape(m, d)
    pos = jnp.arange(t_len)

    h = _ffn(h, bf(ffn1_w_in[0]), bf(ffn1_w_out[0]), ln_g[0, 0], ln_b[0, 0])
    dk = d // RET_HEADS
    dv = 2 * d // RET_HEADS
    half = dk // 2
    inv_freq = jnp.power(jnp.float32(RET_ROPE_BASE), -jnp.arange(half, dtype=F32) / half)
    ang = pos.astype(F32)[:, None] * inv_freq[None, :]
    w_ret = bf(ret_w_in[0])
    n_qkv = 2 * RET_HEADS * dk + RET_HEADS * dv
    qkv = _ret_qkv(h, w_ret, jnp.cos(ang), jnp.sin(ang), t_len, dk, n_qkv)
    gate = _proj(h, w_ret, n_qkv, RET_HEADS * dv, F32, name="ret_gate_proj")
    y = _retention(qkv, gate, ret_gn_g[0], ret_gn_b[0], bsz, t_len, dk, dv)
    h = _out_proj(y, bf(ret_w_out[0]), h, ln_g[0, 1], ln_b[0, 1], "ret_out_proj")
    h = _ffn(h, bf(ffn2_w_in[0]), bf(ffn2_w_out[0]), ln_g[0, 2], ln_b[0, 2])

    hd = d // NSA_HEADS
    rope_dims = hd // 4
    rope_half = rope_dims // 2
    c_tab, s_tab = _rope_tables(pos, rope_dims, ROPE_THETA)
    w_kv = bf(kv_w)
    slot_cols = NSA_KV_GROUPS * hd
    raw_cmp = _head_proj(h, w_kv, c_tab, s_tab, t_len, col0=0, nslots=2, out_dtype=F32, half=rope_half,
                         scale=1.0, rotate="none", split=True, name="kv_cmp_proj")
    kv_rest = _head_proj(h, w_kv, c_tab, s_tab, t_len, col0=2 * slot_cols, nslots=4, out_dtype=BF16,
                         half=rope_half, scale=1.0, rotate="even", split=True, name="kv_slc_win_proj")
    n_rows = t_len // CMP_STRIDE
    cmp_end = jnp.arange(n_rows) * CMP_STRIDE + CMP_BLOCK - 1
    cc_tab, cs_tab = _rope_tables(cmp_end, rope_dims, ROPE_THETA)
    kv_cmp = _cmp_mlp(raw_cmp, cmp_pos, bf(cmp_w1), cmp_b1, bf(cmp_w2), cc_tab, cs_tab, bsz, t_len, rope_half)

    h = _ffn(h, bf(ffn1_w_in[1]), bf(ffn1_w_out[1]), ln_g[1, 0], ln_b[1, 0])
    n_q = NSA_HEADS * hd
    w_q = bf(nsa_w_q[0][:, :n_q])
    q = _head_proj(h, w_q, c_tab, s_tab, t_len, col0=0, nslots=n_q // 512, out_dtype=BF16, half=rope_half,
                   scale=hd ** -0.5, rotate="all", split=False, name="nsa_q_proj")
    w_gate = nsa_w_q[0][:, n_q:].reshape(d, NSA_KV_GROUPS, NSA_HEADS_PER_GROUP * 3)
    w_gate = jnp.pad(w_gate, ((0, 0), (0, 0), (0, LANES - NSA_HEADS_PER_GROUP * 3))).reshape(d, NSA_KV_GROUPS * LANES)
    gates = _proj(h, bf(w_gate), 0, NSA_KV_GROUPS * LANES, F32, act="sigmoid", name="nsa_gate_proj")
    o = _nsa_attention(q, gates, kv_cmp, kv_rest, bsz, t_len)
    h = _out_proj(o, bf(nsa_w_out[0]), h, ln_g[1, 1], ln_b[1, 1], "nsa_out_proj")
    h = _ffn(h, bf(ffn2_w_in[1]), bf(ffn2_w_out[1]), ln_g[1, 2], ln_b[1, 2])
    return h.reshape(bsz, t_len, d)
```

```python
import functools
import math

import numpy as np
import jax
import jax.numpy as jnp
from jax import lax
from jax.experimental import pallas as pl
from jax.experimental.pallas import tpu as pltpu

F32 = jnp.float32
BF16 = jnp.bfloat16

DEPTH = 2
N_A_LAYERS = DEPTH // 2
DEEPNORM_ALPHA = (2 * DEPTH) ** 0.25
LN_EPS = 1e-5
MACARON_WEIGHT = 0.5

RET_HEADS = 8
RET_CHUNK = 256
RET_ROPE_BASE = 10000.0

NSA_HEADS = 16
NSA_KV_GROUPS = 4
NSA_HEADS_PER_GROUP = NSA_HEADS // NSA_KV_GROUPS
CMP_BLOCK = 32
CMP_STRIDE = 16
SLC_BLOCK = 64
SLC_TOP = 8
WINDOW = 512
FORCE_BONUS = 1e4
ROPE_THETA = 500000.0
NEG_INF = -1e30
LOG2_E = math.log2(math.e)

V7X_VMEM_BYTES = 64 * 1024 * 1024
VMEM_LIMIT = V7X_VMEM_BYTES - 8 * 1024 * 1024
LANES = 128

NSA_Q_TILE = 256
NSA_KEY_CHUNK = 512


def _params(*semantics):
    return pltpu.CompilerParams(dimension_semantics=semantics, vmem_limit_bytes=VMEM_LIMIT)


def _layer_norm(y, g, b):
    mu = jnp.mean(y, axis=-1, keepdims=True)
    d = y - mu
    var = jnp.mean(d * d, axis=-1, keepdims=True)
    return d * lax.rsqrt(var + LN_EPS) * g + b


def _dot(a, b):
    return jnp.dot(a, b, preferred_element_type=F32)


def _dot_nt(a, b):
    return lax.dot_general(a, b, (((1,), (1,)), ((), ())), preferred_element_type=F32)


def _dot_tn(a, b):
    return lax.dot_general(a, b, (((0,), (0,)), ((), ())), preferred_element_type=F32)


def _ffn_kernel(x_ref, wa_ref, wu_ref, wo_ref, g_ref, b_ref, o_ref, xb_ref, *, nf):
    f = pl.program_id(1)

    @pl.when(f == 0)
    def _():
        xb_ref[...] = x_ref[...].astype(BF16)
        o_ref[...] = jnp.zeros_like(o_ref)

    xb = xb_ref[...]
    a = _dot(xb, wa_ref[...])
    u = _dot(xb, wu_ref[...])
    h = (a * jax.nn.sigmoid(a)) * u
    o_ref[...] += _dot(h.astype(BF16), wo_ref[...])

    @pl.when(f == nf - 1)
    def _():
        y = DEEPNORM_ALPHA * x_ref[...] + MACARON_WEIGHT * o_ref[...]
        o_ref[...] = _layer_norm(y, g_ref[...], b_ref[...])


def _ffn(x, w_in, w_out, layer, g, b):
    m, d = x.shape
    f_dim = w_out.shape[1]
    tm, tf = 512, 512
    nf = f_dim // tf
    return pl.pallas_call(
        functools.partial(_ffn_kernel, nf=nf),
        grid=(m // tm, nf),
        in_specs=[
            pl.BlockSpec((tm, d), lambda i, f: (i, 0)),
            pl.BlockSpec((None, d, tf), lambda i, f: (layer, 0, f)),
            pl.BlockSpec((None, d, tf), lambda i, f: (layer, 0, f + nf)),
            pl.BlockSpec((None, tf, d), lambda i, f: (layer, f, 0)),
            pl.BlockSpec((1, d), lambda i, f: (0, 0)),
            pl.BlockSpec((1, d), lambda i, f: (0, 0)),
        ],
        out_specs=pl.BlockSpec((tm, d), lambda i, f: (i, 0)),
        out_shape=jax.ShapeDtypeStruct((m, d), F32),
        scratch_shapes=[pltpu.VMEM((tm, d), BF16)],
        compiler_params=_params("parallel", "arbitrary"),
        name="ffn_deepnorm",
    )(x, w_in, w_in, w_out, g.reshape(1, d), b.reshape(1, d))


def _proj_kernel(x_ref, w_ref, o_ref, xb_ref, *, act):
    @pl.when(pl.program_id(1) == 0)
    def _():
        xb_ref[...] = x_ref[...].astype(BF16)

    acc = _dot(xb_ref[...], w_ref[...])
    if act == "sigmoid":
        acc = jax.nn.sigmoid(acc)
    o_ref[...] = acc.astype(o_ref.dtype)


def _proj(x, w, col0, ncols, out_dtype, act=None, name="proj"):
    m, k = x.shape
    tm, tn = 1024, min(1024, ncols)
    assert col0 % tn == 0 and ncols % tn == 0
    return pl.pallas_call(
        functools.partial(_proj_kernel, act=act),
        grid=(m // tm, ncols // tn),
        in_specs=[
            pl.BlockSpec((tm, k), lambda i, j: (i, 0)),
            pl.BlockSpec((None, k, tn), lambda i, j: (0, 0, j + col0 // tn)),
        ],
        out_specs=pl.BlockSpec((tm, tn), lambda i, j: (i, j)),
        out_shape=jax.ShapeDtypeStruct((m, ncols), out_dtype),
        scratch_shapes=[pltpu.VMEM((tm, k), BF16)],
        compiler_params=_params("parallel", "arbitrary"),
        name=name,
    )(x, w)


def _ret_qkv_kernel(x_ref, w_ref, cos_ref, sin_ref, o_ref, xb_ref, *, n_rot, k_start, k_scale, half):
    j = pl.program_id(1)

    @pl.when(j == 0)
    def _():
        xb_ref[...] = x_ref[...].astype(BF16)

    acc = _dot(xb_ref[...], w_ref[...])
    tn = acc.shape[1]

    @pl.when(j < n_rot)
    def _():
        scale = jnp.where(j >= k_start, k_scale, 1.0).astype(F32)
        cos = cos_ref[...]
        sin = sin_ref[...]
        for h in range(tn // (2 * half)):
            c0 = h * 2 * half
            x1 = acc[:, c0:c0 + half]
            x2 = acc[:, c0 + half:c0 + 2 * half]
            o_ref[:, c0:c0 + half] = ((x1 * cos - x2 * sin) * scale).astype(o_ref.dtype)
            o_ref[:, c0 + half:c0 + 2 * half] = ((x2 * cos + x1 * sin) * scale).astype(o_ref.dtype)

    @pl.when(j >= n_rot)
    def _():
        o_ref[...] = acc.astype(o_ref.dtype)


def _ret_qkv(x, w, cos, sin, t_len, dk, ncols):
    m, k = x.shape
    tm, tn = 1024, 1024
    qk_tiles = RET_HEADS * dk // tn
    kern = functools.partial(_ret_qkv_kernel, n_rot=2 * qk_tiles, k_start=qk_tiles,
                             k_scale=dk ** -0.5, half=dk // 2)
    tpb = t_len // tm
    return pl.pallas_call(
        kern,
        grid=(m // tm, ncols // tn),
        in_specs=[
            pl.BlockSpec((tm, k), lambda i, j: (i, 0)),
            pl.BlockSpec((None, k, tn), lambda i, j: (0, 0, j)),
            pl.BlockSpec((tm, dk // 2), lambda i, j: (i % tpb, 0)),
            pl.BlockSpec((tm, dk // 2), lambda i, j: (i % tpb, 0)),
        ],
        out_specs=pl.BlockSpec((tm, tn), lambda i, j: (i, j)),
        out_shape=jax.ShapeDtypeStruct((m, ncols), BF16),
        scratch_shapes=[pltpu.VMEM((tm, k), BF16)],
        compiler_params=_params("parallel", "arbitrary"),
        name="ret_qkv_proj",
    )(x, w, cos, sin)


def _partial_rotary(x, c, s, half):
    lane = lax.broadcasted_iota(jnp.int32, x.shape, 1)
    partner = jnp.where(lane < half, pltpu.roll(x, LANES - half, 1), pltpu.roll(x, half, 1))
    return x * c + partner * s


def _head_proj_kernel(x_ref, w_ref, c_ref, s_ref, o_ref, xb_ref, *, hd, hps, half, scale, rotate, split):
    @pl.when(pl.program_id(1) == 0)
    def _():
        xb_ref[...] = x_ref[...].astype(BF16)

    acc = _dot(xb_ref[...], w_ref[...])
    for h in range(acc.shape[1] // hd):
        piece = acc[:, h * hd:(h + 1) * hd]
        if rotate == "all" or (rotate == "even" and h < hps):
            piece = _partial_rotary(piece, c_ref[...], s_ref[...], half)
        if scale != 1.0:
            piece = piece * scale
        if split:
            o_ref[h // hps, h % hps] = piece.astype(o_ref.dtype)
        else:
            o_ref[:, h * hd:(h + 1) * hd] = piece.astype(o_ref.dtype)


def _head_proj(x, w, c_tab, s_tab, t_len, *, col0, nslots, out_dtype, half, scale, rotate, split, name):
    m, k = x.shape
    hd, hps = LANES, NSA_KV_GROUPS
    tm, tn = 1024, 2 * hps * hd
    tpb = t_len // tm
    assert nslots % 2 == 0 and col0 % tn == 0
    kern = functools.partial(_head_proj_kernel, hd=hd, hps=hps, half=half, scale=scale, rotate=rotate, split=split)
    if split:
        out_spec = pl.BlockSpec((2, hps, tm, hd), lambda i, j: (j, 0, i, 0))
        out_shape = jax.ShapeDtypeStruct((nslots, hps, m, hd), out_dtype)
    else:
        out_spec = pl.BlockSpec((tm, tn), lambda i, j: (i, j))
        out_shape = jax.ShapeDtypeStruct((m, nslots * hps * hd), out_dtype)
    return pl.pallas_call(
        kern,
        grid=(m // tm, nslots // 2),
        in_specs=[
            pl.BlockSpec((tm, k), lambda i, j: (i, 0)),
            pl.BlockSpec((None, k, tn), lambda i, j: (0, 0, j + col0 // tn)),
            pl.BlockSpec((tm, hd), lambda i, j: (i % tpb, 0)),
            pl.BlockSpec((tm, hd), lambda i, j: (i % tpb, 0)),
        ],
        out_specs=out_spec,
        out_shape=out_shape,
        scratch_shapes=[pltpu.VMEM((tm, k), BF16)],
        compiler_params=_params("parallel", "arbitrary"),
        name=name,
    )(x, w, c_tab, s_tab)


def _out_proj_kernel(y_ref, w_ref, x_ref, g_ref, b_ref, o_ref, *, nk):
    kk = pl.program_id(1)

    @pl.when(kk == 0)
    def _():
        o_ref[...] = jnp.zeros_like(o_ref)

    o_ref[...] += _dot(y_ref[...], w_ref[...])

    @pl.when(kk == nk - 1)
    def _():
        o_ref[...] = _layer_norm(DEEPNORM_ALPHA * x_ref[...] + o_ref[...], g_ref[...], b_ref[...])


def _out_proj(y, w, x, g, b, name):
    m, k = y.shape
    d = w.shape[2]
    tm, tk = 512, 2048
    nk = k // tk
    return pl.pallas_call(
        functools.partial(_out_proj_kernel, nk=nk),
        grid=(m // tm, nk),
        in_specs=[
            pl.BlockSpec((tm, tk), lambda i, kk: (i, kk)),
            pl.BlockSpec((None, tk, d), lambda i, kk: (0, kk, 0)),
            pl.BlockSpec((tm, d), lambda i, kk: (i, 0)),
            pl.BlockSpec((1, d), lambda i, kk: (0, 0)),
            pl.BlockSpec((1, d), lambda i, kk: (0, 0)),
        ],
        out_specs=pl.BlockSpec((tm, d), lambda i, kk: (i, 0)),
        out_shape=jax.ShapeDtypeStruct((m, d), F32),
        compiler_params=_params("parallel", "arbitrary"),
        name=name,
    )(y, w, x, g.reshape(1, d), b.reshape(1, d))


def _retention_kernel(gam_ref, q_ref, k_ref, v_ref, g_ref, dec_ref, xi_ref, zeta_ref, gng_ref, gnb_ref,
                      o_ref, state_ref, *, nh, dk, dv):
    @pl.when(pl.program_id(1) == 0)
    def _():
        state_ref[...] = jnp.zeros_like(state_ref)

    for h in range(nh):
        q = q_ref[:, h * dk:(h + 1) * dk]
        k = k_ref[:, h * dk:(h + 1) * dk]
        v = v_ref[:, h * dv:(h + 1) * dv]
        state = state_ref[h]
        s = _dot_nt(q, k) * dec_ref[h]
        out = _dot(s.astype(BF16), v) + _dot(q, state.astype(BF16)) * xi_ref[h]
        kz = (k.astype(F32) * zeta_ref[h]).astype(BF16)
        state_ref[h] = _dot_tn(kz, v) + gam_ref[h] * state

        mu = jnp.mean(out, axis=-1, keepdims=True)
        d = out - mu
        var = jnp.mean(d * d, axis=-1, keepdims=True)
        cols = slice(h * dv, (h + 1) * dv)
        gn = d * lax.rsqrt(var + LN_EPS) * gng_ref[:, cols] + gnb_ref[:, cols]
        gate = g_ref[:, cols]
        o_ref[:, cols] = ((gate * jax.nn.sigmoid(gate)) * gn).astype(o_ref.dtype)


def _retention(qkv, gate, gn_g, gn_b, bsz, t_len, dk, dv):
    h = RET_HEADS
    c = RET_CHUNK
    nc = t_len // c
    m = bsz * t_len
    log_gamma = jnp.log(1.0 - jnp.power(2.0, -5.0 - jnp.arange(h, dtype=F32)))
    idx = jnp.arange(c, dtype=F32)
    diff = idx[:, None] - idx[None, :]
    decay = jnp.where(diff >= 0, jnp.exp(log_gamma[:, None, None] * jnp.maximum(diff, 0.0)), 0.0)
    xi = jnp.exp(log_gamma[:, None] * (idx + 1.0))[..., None]
    zeta = jnp.exp(log_gamma[:, None] * (c - 1.0 - idx))[..., None]
    gamma_c = jnp.exp(log_gamma * c)
    row = lambda b, cc: b * nc + cc
    whole = lambda shape: pl.BlockSpec(shape, lambda b, cc: (0,) * len(shape))
    return pl.pallas_call(
        functools.partial(_retention_kernel, nh=h, dk=dk, dv=dv),
        grid=(bsz, nc),
        in_specs=[
            pl.BlockSpec(memory_space=pltpu.SMEM),
            pl.BlockSpec((c, h * dk), lambda b, cc: (row(b, cc), 0)),
            pl.BlockSpec((c, h * dk), lambda b, cc: (row(b, cc), 1)),
            pl.BlockSpec((c, h * dv), lambda b, cc: (row(b, cc), 2 * dk // dv)),
            pl.BlockSpec((c, h * dv), lambda b, cc: (row(b, cc), 0)),
            whole((h, c, c)), whole((h, c, 1)), whole((h, c, 1)),
            whole((1, h * dv)), whole((1, h * dv)),
        ],
        out_specs=pl.BlockSpec((c, h * dv), lambda b, cc: (row(b, cc), 0)),
        out_shape=jax.ShapeDtypeStruct((m, h * dv), BF16),
        scratch_shapes=[pltpu.VMEM((h, dk, dv), F32)],
        compiler_params=_params("parallel", "arbitrary"),
        name="retention_chunks",
    )(gamma_c, qkv, qkv, qkv, gate, decay, xi, zeta, gn_g.reshape(1, h * dv), gn_b.reshape(1, h * dv))


def _cmp_mlp_kernel(x_ref, pos_ref, w1a_ref, w1b_ref, b1_ref, w2_ref, c_ref, s_ref, o_ref, *, half):
    x = x_ref[...]
    pos = pos_ref[...]
    a = _dot((x + pos[0:1, :]).astype(BF16), w1a_ref[...])
    bm = _dot((x + pos[1:2, :]).astype(BF16), w1b_ref[...])
    nrows = x.shape[0]
    hid = jax.nn.gelu(a + pltpu.roll(bm, nrows - 1, 0) + b1_ref[...])
    comp = _dot(hid.astype(BF16), w2_ref[...])
    is_key = pl.program_id(0) == 0
    c = jnp.where(is_key, c_ref[...], 1.0)
    s = jnp.where(is_key, s_ref[...], 0.0)
    o_ref[...] = _partial_rotary(comp, c, s, half).astype(o_ref.dtype)


def _cmp_mlp(raw, cmp_pos, w1, b1, w2, c_tab, s_tab, bsz, t_len, half):
    g, hd = NSA_KV_GROUPS, LANES
    nrows = t_len // CMP_STRIDE
    feat = CMP_STRIDE * hd
    hidden = w1.shape[-1]
    x = raw.reshape(2, g, bsz, nrows, feat)
    pos = cmp_pos.reshape(2, CMP_BLOCK // CMP_STRIDE, feat)
    return pl.pallas_call(
        functools.partial(_cmp_mlp_kernel, half=half),
        grid=(2, g, bsz),
        in_specs=[
            pl.BlockSpec((None, None, None, nrows, feat), lambda c, gg, b: (c, gg, b, 0, 0)),
            pl.BlockSpec((None, 2, feat), lambda c, gg, b: (c, 0, 0)),
            pl.BlockSpec((None, feat, hidden), lambda c, gg, b: (c, 0, 0)),
            pl.BlockSpec((None, feat, hidden), lambda c, gg, b: (c, 1, 0)),
            pl.BlockSpec((None, 1, hidden), lambda c, gg, b: (c, 0, 0)),
            pl.BlockSpec((None, hidden, hd), lambda c, gg, b: (c, 0, 0)),
            pl.BlockSpec((nrows, hd), lambda c, gg, b: (0, 0)),
            pl.BlockSpec((nrows, hd), lambda c, gg, b: (0, 0)),
        ],
        out_specs=pl.BlockSpec((None, None, None, nrows, hd), lambda c, gg, b: (c, gg, b, 0, 0)),
        out_shape=jax.ShapeDtypeStruct((2, g, bsz, nrows, hd), BF16),
        compiler_params=_params("arbitrary", "arbitrary", "arbitrary"),
        name="cmp_mlp",
    )(x, pos, w1, w1, b1.reshape(2, 1, hidden), w2, c_tab, s_tab)


def _with_ones(v):
    return jnp.concatenate([v, jnp.ones_like(v)], axis=1)


def _nsa_kernel(q_ref, gate_ref, kc_ref, vc_ref, ks_ref, vs_ref, kw_ref, vw_ref, ovt_ref, ex_ref,
                o_ref, mask_ref, oslc_ref, *, tq, tc, hg, hd, max_chunks):
    qi = pl.program_id(2)
    t0 = qi * tq
    rows = hg * tq
    q = q_ref[...]
    q4 = jnp.concatenate([q[:, i * hd:(i + 1) * hd] for i in range(hg)], axis=0)
    tpos = t0 + lax.broadcasted_iota(jnp.int32, (tq, 1), 0)

    ncmp = kc_ref.shape[0]
    s3 = _dot_nt(q4, kc_ref[...]).reshape(hg, tq, ncmp)
    cmp_end = lax.broadcasted_iota(jnp.int32, (tq, ncmp), 1) * CMP_STRIDE + (CMP_BLOCK - 1)
    s3 = jnp.where((cmp_end <= tpos)[None], s3, NEG_INF)
    e3 = jnp.exp2(s3 - jnp.max(s3, axis=-1, keepdims=True))
    has_cmp = (tpos >= CMP_BLOCK - 1).astype(F32)
    p3 = (e3 / jnp.sum(e3, axis=-1, keepdims=True) * has_cmp[None]).astype(BF16)
    o_cmp = _dot(p3.reshape(rows, ncmp), vc_ref[...])

    nslc = ovt_ref.shape[0]
    p_slc = _dot_nt(ovt_ref[...], jnp.concatenate([p3[i] for i in range(hg)], axis=1))
    trow = t0 + lax.broadcasted_iota(jnp.int32, (1, tq), 1)
    blk = lax.broadcasted_iota(jnp.int32, (nslc, tq), 0)
    cur = jnp.right_shift(trow, SLC_BLOCK.bit_length() - 1)
    valid = blk * SLC_BLOCK <= trow
    forced = (blk == 0) | (blk == cur) | (blk == cur - 1)
    score = jnp.where(valid, p_slc + jnp.where(forced, FORCE_BONUS, 0.0), -1.0)
    rank = jnp.zeros((nslc, tq), jnp.int32)
    for kb in range(nslc):
        ck = score[kb:kb + 1, :]
        rank += ((ck > score) | ((ck == score) & (blk > kb))).astype(jnp.int32)
    sel = jnp.where(rank < SLC_TOP, 1.0, 0.0).astype(BF16)
    mask_ref[...] = _dot_tn(sel, ex_ref[...])

    wlen = WINDOW + tq
    w0 = pl.multiple_of(jnp.maximum(qi - WINDOW // tq, 0) * tq, tq)
    sw = _dot_nt(q4, kw_ref[pl.ds(w0, wlen), :]).reshape(hg, tq, wlen)
    kpos = w0 + lax.broadcasted_iota(jnp.int32, (tq, wlen), 1)
    okw = (kpos <= tpos) & (kpos > tpos - WINDOW)
    sw = jnp.where(okw[None], sw, NEG_INF)
    ew = jnp.exp2(sw - jnp.max(sw, axis=-1, keepdims=True)).astype(BF16)
    rw = _dot(ew.reshape(rows, wlen), _with_ones(vw_ref[pl.ds(w0, wlen), :]))
    o_win = rw[:, :hd] / rw[:, hd:]

    def selected(n_chunks):
        m_i = acc = None
        for c in range(n_chunks):
            k0 = c * tc
            s = _dot_nt(q4, ks_ref[k0:k0 + tc, :]).reshape(hg, tq, tc)
            ok = mask_ref[:, k0:k0 + tc] > 0.5
            if c == n_chunks - 1:
                ok &= (k0 + lax.broadcasted_iota(jnp.int32, (tq, tc), 1)) <= tpos
            s = jnp.where(ok[None], s, NEG_INF)
            m_c = jnp.max(s, axis=-1, keepdims=True)
            m_new = m_c if c == 0 else jnp.maximum(m_i, m_c)
            pv = _dot(jnp.exp2(s - m_new).astype(BF16).reshape(rows, tc), _with_ones(vs_ref[k0:k0 + tc, :]))
            acc = pv if c == 0 else acc * jnp.exp2(m_i - m_new).reshape(rows, 1) + pv
            m_i = m_new
        oslc_ref[...] = acc[:, :hd] / acc[:, hd:]

    n_chunks = lax.div(t0, tc) + 1
    for n in range(1, max_chunks + 1):
        pl.when(n_chunks == n)(functools.partial(selected, n))
    o_slc = oslc_ref[...]

    gts = gate_ref[...]
    for i in range(hg):
        r = slice(i * tq, (i + 1) * tq)
        o = (gts[:, 3 * i:3 * i + 1] * o_cmp[r] + gts[:, 3 * i + 1:3 * i + 2] * o_slc[r]
             + gts[:, 3 * i + 2:3 * i + 3] * o_win[r])
        o_ref[:, i * hd:(i + 1) * hd] = o.astype(o_ref.dtype)


def _nsa_attention(q, gates, kv_cmp, kv_rest, bsz, t_len):
    g, hg, hd = NSA_KV_GROUPS, NSA_HEADS_PER_GROUP, LANES
    tq, tc = NSA_Q_TILE, NSA_KEY_CHUNK
    nq = t_len // tq
    ncmp_pad = t_len // CMP_STRIDE
    n_cmp = (t_len - CMP_BLOCK) // CMP_STRIDE + 1
    n_slc = t_len // SLC_BLOCK
    cs = np.arange(ncmp_pad)[None, :] * CMP_STRIDE
    ss = np.arange(n_slc)[:, None] * SLC_BLOCK
    ovt = ((cs < ss + SLC_BLOCK) & (cs + CMP_BLOCK > ss) & (np.arange(ncmp_pad)[None, :] < n_cmp)).astype(np.float32)
    ovt = jnp.asarray(np.tile(ovt, (1, hg)), BF16)
    expand = jnp.asarray((np.arange(t_len)[None, :] // SLC_BLOCK == np.arange(n_slc)[:, None]).astype(np.float32), BF16)
    kv_spec = lambda slot: pl.BlockSpec((None, None, t_len, hd), lambda b, gg, qi: (slot, gg, b, 0))
    cmp_spec = lambda slot: pl.BlockSpec((None, None, None, ncmp_pad, hd), lambda b, gg, qi: (slot, gg, b, 0, 0))
    return pl.pallas_call(
        functools.partial(_nsa_kernel, tq=tq, tc=tc, hg=hg, hd=hd, max_chunks=t_len // tc),
        grid=(bsz, g, nq),
        in_specs=[
            pl.BlockSpec((tq, hg * hd), lambda b, gg, qi: (b * nq + qi, gg)),
            pl.BlockSpec((tq, LANES), lambda b, gg, qi: (b * nq + qi, gg)),
            cmp_spec(0), cmp_spec(1),
            kv_spec(0), kv_spec(1), kv_spec(2), kv_spec(3),
            pl.BlockSpec((n_slc, hg * ncmp_pad), lambda b, gg, qi: (0, 0)),
            pl.BlockSpec((n_slc, t_len), lambda b, gg, qi: (0, 0)),
        ],
        out_specs=pl.BlockSpec((tq, hg * hd), lambda b, gg, qi: (b * nq + qi, gg)),
        out_shape=jax.ShapeDtypeStruct((bsz * t_len, g * hg * hd), BF16),
        scratch_shapes=[pltpu.VMEM((tq, t_len), F32), pltpu.VMEM((hg * tq, hd), F32)],
        compiler_params=_params("parallel", "parallel", "arbitrary"),
        name="nsa_attention",
    )(q, gates, kv_cmp, kv_cmp, kv_rest, kv_rest, kv_rest, kv_rest, ovt, expand)


def _rope_tables(pos, rot_dims, theta):
    half = rot_dims // 2
    inv_freq = jnp.power(jnp.float32(theta), -jnp.arange(half, dtype=F32) / half)
    ang = pos.astype(F32)[:, None] * inv_freq[None, :]
    cos, sin = jnp.cos(ang), jnp.sin(ang)
    n = pos.shape[0]
    pad = LANES - rot_dims
    c_tab = jnp.concatenate([cos, cos, jnp.ones((n, pad), F32)], axis=1)
    s_tab = jnp.concatenate([-sin, sin, jnp.zeros((n, pad), F32)], axis=1)
    return c_tab, s_tab


def kernel(x, ln_g, ln_b, ffn1_w_in, ffn1_w_out, ffn2_w_in, ffn2_w_out, ret_w_in, ret_gn_g, ret_gn_b, ret_w_out, kv_w, cmp_pos, cmp_w1, cmp_b1, cmp_w2, nsa_w_q, nsa_w_out):
    bsz, t_len, d = x.shape
    m = bsz * t_len
    bf = lambda w: w.astype(BF16)
    h = x.reshape(m, d)
    pos = jnp.arange(t_len)
    ffn1_in, ffn1_out, ffn2_in, ffn2_out = bf(ffn1_w_in), bf(ffn1_w_out), bf(ffn2_w_in), bf(ffn2_w_out)

    h = _ffn(h, ffn1_in, ffn1_out, 0, ln_g[0, 0], ln_b[0, 0])
    dk = d // RET_HEADS
    dv = 2 * d // RET_HEADS
    half = dk // 2
    inv_freq = jnp.power(jnp.float32(RET_ROPE_BASE), -jnp.arange(half, dtype=F32) / half)
    ang = pos.astype(F32)[:, None] * inv_freq[None, :]
    w_ret = bf(ret_w_in)
    n_qkv = 2 * RET_HEADS * dk + RET_HEADS * dv
    qkv = _ret_qkv(h, w_ret, jnp.cos(ang), jnp.sin(ang), t_len, dk, n_qkv)
    gate = _proj(h, w_ret, n_qkv, RET_HEADS * dv, F32, name="ret_gate_proj")
    y = _retention(qkv, gate, ret_gn_g[0], ret_gn_b[0], bsz, t_len, dk, dv)
    h = _out_proj(y, bf(ret_w_out), h, ln_g[0, 1], ln_b[0, 1], "ret_out_proj")
    h = _ffn(h, ffn2_in, ffn2_out, 0, ln_g[0, 2], ln_b[0, 2])

    hd = d // NSA_HEADS
    rope_dims = hd // 4
    rope_half = rope_dims // 2
    c_tab, s_tab = _rope_tables(pos, rope_dims, ROPE_THETA)
    w_kv = bf(kv_w)[None]
    slot_cols = NSA_KV_GROUPS * hd
    raw_cmp = _head_proj(h, w_kv, c_tab, s_tab, t_len, col0=0, nslots=2, out_dtype=F32, half=rope_half,
                         scale=1.0, rotate="none", split=True, name="kv_cmp_proj")
    kv_rest = _head_proj(h, w_kv, c_tab, s_tab, t_len, col0=2 * slot_cols, nslots=4, out_dtype=BF16,
                         half=rope_half, scale=1.0, rotate="even", split=True, name="kv_slc_win_proj")
    n_rows = t_len // CMP_STRIDE
    cmp_end = jnp.arange(n_rows) * CMP_STRIDE + CMP_BLOCK - 1
    cc_tab, cs_tab = _rope_tables(cmp_end, rope_dims, ROPE_THETA)
    kv_cmp = _cmp_mlp(raw_cmp, cmp_pos, bf(cmp_w1), cmp_b1, bf(cmp_w2), cc_tab, cs_tab, bsz, t_len, rope_half)

    h = _ffn(h, ffn1_in, ffn1_out, 1, ln_g[1, 0], ln_b[1, 0])
    n_q = NSA_HEADS * hd
    q = _head_proj(h, bf(nsa_w_q), c_tab, s_tab, t_len, col0=0, nslots=n_q // 512, out_dtype=BF16,
                   half=rope_half, scale=hd ** -0.5 * LOG2_E, rotate="all", split=False, name="nsa_q_proj")
    w_gate = nsa_w_q[0][:, n_q:].reshape(d, NSA_KV_GROUPS, NSA_HEADS_PER_GROUP * 3)
    w_gate = jnp.pad(w_gate, ((0, 0), (0, 0), (0, LANES - NSA_HEADS_PER_GROUP * 3))).reshape(1, d, NSA_KV_GROUPS * LANES)
    gates = _proj(h, bf(w_gate), 0, NSA_KV_GROUPS * LANES, F32, act="sigmoid", name="nsa_gate_proj")
    o = _nsa_attention(q, gates, kv_cmp, kv_rest, bsz, t_len)
    h = _out_proj(o, bf(nsa_w_out), h, ln_g[1, 1], ln_b[1, 1], "nsa_out_proj")
    h = _ffn(h, ffn2_in, ffn2_out, 1, ln_g[1, 2], ln_b[1, 2])
    return h.reshape(bsz, t_len, d)
```

```python
import functools
import math

import numpy as np
import jax
import jax.numpy as jnp
from jax import lax
from jax.experimental import pallas as pl
from jax.experimental.pallas import tpu as pltpu

F32 = jnp.float32
BF16 = jnp.bfloat16

DEPTH = 2
N_A_LAYERS = DEPTH // 2
DEEPNORM_ALPHA = (2 * DEPTH) ** 0.25
LN_EPS = 1e-5
MACARON_WEIGHT = 0.5

RET_HEADS = 8
RET_CHUNK = 256
RET_ROPE_BASE = 10000.0

NSA_HEADS = 16
NSA_KV_GROUPS = 4
NSA_HEADS_PER_GROUP = NSA_HEADS // NSA_KV_GROUPS
CMP_BLOCK = 32
CMP_STRIDE = 16
SLC_BLOCK = 64
SLC_TOP = 8
WINDOW = 512
FORCE_BONUS = 1e4
ROPE_THETA = 500000.0
NEG_INF = -1e30
LOG2_E = math.log2(math.e)

V7X_VMEM_BYTES = 64 * 1024 * 1024
VMEM_LIMIT = V7X_VMEM_BYTES - 8 * 1024 * 1024
LANES = 128

FFN_SUBTILES = 2
NSA_Q_TILE = 256
NSA_KEY_CHUNK = 512


def _params(*semantics):
    return pltpu.CompilerParams(dimension_semantics=semantics, vmem_limit_bytes=VMEM_LIMIT)


def _layer_norm(y, g, b):
    mu = jnp.mean(y, axis=-1, keepdims=True)
    d = y - mu
    var = jnp.mean(d * d, axis=-1, keepdims=True)
    return d * lax.rsqrt(var + LN_EPS) * g + b


def _dot(a, b):
    return jnp.dot(a, b, preferred_element_type=F32)


def _dot_nt(a, b):
    return lax.dot_general(a, b, (((1,), (1,)), ((), ())), preferred_element_type=F32)


def _dot_tn(a, b):
    return lax.dot_general(a, b, (((0,), (0,)), ((), ())), preferred_element_type=F32)


def _ffn_kernel(x_ref, wa_ref, wu_ref, wo_ref, g_ref, b_ref, *rest, nf, cast_next):
    if cast_next:
        nwi_ref, nwo_ref, o_ref, nwi_out, nwo_out, xb_ref = rest
        nwi_out[...] = nwi_ref[...].astype(BF16)
        nwo_out[...] = nwo_ref[...].astype(BF16)
    else:
        o_ref, xb_ref = rest
    f = pl.program_id(1)

    @pl.when(f == 0)
    def _():
        xb_ref[...] = x_ref[...].astype(BF16)
        o_ref[...] = jnp.zeros_like(o_ref)

    xb = xb_ref[...]
    tf = wa_ref.shape[1]
    sub = tf // FFN_SUBTILES
    part = None
    for s in range(FFN_SUBTILES):
        a = _dot(xb, wa_ref[:, s * sub:(s + 1) * sub])
        u = _dot(xb, wu_ref[:, s * sub:(s + 1) * sub])
        h = (a * jax.nn.sigmoid(a)) * u
        p = _dot(h.astype(BF16), wo_ref[s * sub:(s + 1) * sub, :])
        part = p if part is None else part + p
    o_ref[...] += part

    @pl.when(f == nf - 1)
    def _():
        y = DEEPNORM_ALPHA * x_ref[...] + MACARON_WEIGHT * o_ref[...]
        o_ref[...] = _layer_norm(y, g_ref[...], b_ref[...])


def _ffn(x, w_in, w_out, g, b, cast_next=None):
    m, d = x.shape
    f_dim = w_out.shape[0]
    tm, tf = 512, 512
    ni, nf = m // tm, f_dim // tf
    in_specs = [
        pl.BlockSpec((tm, d), lambda i, f: (i, 0)),
        pl.BlockSpec((d, tf), lambda i, f: (0, f)),
        pl.BlockSpec((d, tf), lambda i, f: (0, f + nf)),
        pl.BlockSpec((tf, d), lambda i, f: (f, 0)),
        pl.BlockSpec((1, d), lambda i, f: (0, 0)),
        pl.BlockSpec((1, d), lambda i, f: (0, 0)),
    ]
    out_specs = pl.BlockSpec((tm, d), lambda i, f: (i, 0))
    out_shape = jax.ShapeDtypeStruct((m, d), F32)
    args = [x, w_in, w_in, w_out, g.reshape(1, d), b.reshape(1, d)]
    if cast_next is not None:
        nw_in, nw_out, layer = cast_next
        ri, ci = d // ni, 2 * f_dim // nf
        ro = f_dim // (ni * nf)
        assert ri % 16 == 0 and ci % LANES == 0 and ro % 16 == 0
        in_specs += [pl.BlockSpec((None, ri, ci), lambda i, f: (layer, i, f)),
                     pl.BlockSpec((None, ro, d), lambda i, f: (layer, i * nf + f, 0))]
        out_specs = [out_specs, pl.BlockSpec((ri, ci), lambda i, f: (i, f)),
                     pl.BlockSpec((ro, d), lambda i, f: (i * nf + f, 0))]
        out_shape = [out_shape, jax.ShapeDtypeStruct((d, 2 * f_dim), BF16), jax.ShapeDtypeStruct((f_dim, d), BF16)]
        args += [nw_in, nw_out]
    return pl.pallas_call(
        functools.partial(_ffn_kernel, nf=nf, cast_next=cast_next is not None),
        grid=(ni, nf),
        in_specs=in_specs,
        out_specs=out_specs,
        out_shape=out_shape,
        scratch_shapes=[pltpu.VMEM((tm, d), BF16)],
        compiler_params=_params("arbitrary", "arbitrary"),
        name="ffn_deepnorm",
    )(*args)


def _proj_kernel(x_ref, w_ref, o_ref, xb_ref, *, act):
    @pl.when(pl.program_id(1) == 0)
    def _():
        xb_ref[...] = x_ref[...].astype(BF16)

    acc = _dot(xb_ref[...], w_ref[...])
    if act == "sigmoid":
        acc = jax.nn.sigmoid(acc)
    o_ref[...] = acc.astype(o_ref.dtype)


def _proj(x, w, col0, ncols, out_dtype, act=None, name="proj"):
    m, k = x.shape
    tm, tn = 1024, min(1024, ncols)
    assert col0 % tn == 0 and ncols % tn == 0
    return pl.pallas_call(
        functools.partial(_proj_kernel, act=act),
        grid=(m // tm, ncols // tn),
        in_specs=[
            pl.BlockSpec((tm, k), lambda i, j: (i, 0)),
            pl.BlockSpec((None, k, tn), lambda i, j: (0, 0, j + col0 // tn)),
        ],
        out_specs=pl.BlockSpec((tm, tn), lambda i, j: (i, j)),
        out_shape=jax.ShapeDtypeStruct((m, ncols), out_dtype),
        scratch_shapes=[pltpu.VMEM((tm, k), BF16)],
        compiler_params=_params("parallel", "arbitrary"),
        name=name,
    )(x, w)


def _ret_qkv_kernel(x_ref, w_ref, cos_ref, sin_ref, o_ref, xb_ref, *, n_rot, k_start, k_scale, half):
    j = pl.program_id(1)

    @pl.when(j == 0)
    def _():
        xb_ref[...] = x_ref[...].astype(BF16)

    acc = _dot(xb_ref[...], w_ref[...])
    tn = acc.shape[1]

    @pl.when(j < n_rot)
    def _():
        scale = jnp.where(j >= k_start, k_scale, 1.0).astype(F32)
        cos = cos_ref[...]
        sin = sin_ref[...]
        for h in range(tn // (2 * half)):
            c0 = h * 2 * half
            x1 = acc[:, c0:c0 + half]
            x2 = acc[:, c0 + half:c0 + 2 * half]
            o_ref[:, c0:c0 + half] = ((x1 * cos - x2 * sin) * scale).astype(o_ref.dtype)
            o_ref[:, c0 + half:c0 + 2 * half] = ((x2 * cos + x1 * sin) * scale).astype(o_ref.dtype)

    @pl.when(j >= n_rot)
    def _():
        o_ref[...] = acc.astype(o_ref.dtype)


def _ret_qkv(x, w, cos, sin, t_len, dk, ncols):
    m, k = x.shape
    tm, tn = 1024, 1024
    qk_tiles = RET_HEADS * dk // tn
    kern = functools.partial(_ret_qkv_kernel, n_rot=2 * qk_tiles, k_start=qk_tiles,
                             k_scale=dk ** -0.5, half=dk // 2)
    tpb = t_len // tm
    return pl.pallas_call(
        kern,
        grid=(m // tm, ncols // tn),
        in_specs=[
            pl.BlockSpec((tm, k), lambda i, j: (i, 0)),
            pl.BlockSpec((None, k, tn), lambda i, j: (0, 0, j)),
            pl.BlockSpec((tm, dk // 2), lambda i, j: (i % tpb, 0)),
            pl.BlockSpec((tm, dk // 2), lambda i, j: (i % tpb, 0)),
        ],
        out_specs=pl.BlockSpec((tm, tn), lambda i, j: (i, j)),
        out_shape=jax.ShapeDtypeStruct((m, ncols), BF16),
        scratch_shapes=[pltpu.VMEM((tm, k), BF16)],
        compiler_params=_params("parallel", "arbitrary"),
        name="ret_qkv_proj",
    )(x, w, cos, sin)


def _partial_rotary(x, c, s, half):
    lane = lax.broadcasted_iota(jnp.int32, x.shape, 1)
    partner = jnp.where(lane < half, pltpu.roll(x, LANES - half, 1), pltpu.roll(x, half, 1))
    return x * c + partner * s


def _head_proj_kernel(x_ref, w_ref, c_ref, s_ref, o_ref, xb_ref, *, hd, hps, half, scale, rotate, split):
    @pl.when(pl.program_id(1) == 0)
    def _():
        xb_ref[...] = x_ref[...].astype(BF16)

    acc = _dot(xb_ref[...], w_ref[...])
    for h in range(acc.shape[1] // hd):
        piece = acc[:, h * hd:(h + 1) * hd]
        if rotate == "all" or (rotate == "even" and h < hps):
            piece = _partial_rotary(piece, c_ref[...], s_ref[...], half)
        if scale != 1.0:
            piece = piece * scale
        if split:
            o_ref[h // hps, h % hps] = piece.astype(o_ref.dtype)
        else:
            o_ref[:, h * hd:(h + 1) * hd] = piece.astype(o_ref.dtype)


def _head_proj(x, w, c_tab, s_tab, t_len, *, col0, nslots, out_dtype, half, scale, rotate, split, name):
    m, k = x.shape
    hd, hps = LANES, NSA_KV_GROUPS
    tm, tn = 1024, 2 * hps * hd
    tpb = t_len // tm
    assert nslots % 2 == 0 and col0 % tn == 0
    kern = functools.partial(_head_proj_kernel, hd=hd, hps=hps, half=half, scale=scale, rotate=rotate, split=split)
    if split:
        out_spec = pl.BlockSpec((2, hps, tm, hd), lambda i, j: (j, 0, i, 0))
        out_shape = jax.ShapeDtypeStruct((nslots, hps, m, hd), out_dtype)
    else:
        out_spec = pl.BlockSpec((tm, tn), lambda i, j: (i, j))
        out_shape = jax.ShapeDtypeStruct((m, nslots * hps * hd), out_dtype)
    return pl.pallas_call(
        kern,
        grid=(m // tm, nslots // 2),
        in_specs=[
            pl.BlockSpec((tm, k), lambda i, j: (i, 0)),
            pl.BlockSpec((None, k, tn), lambda i, j: (0, 0, j + col0 // tn)),
            pl.BlockSpec((tm, hd), lambda i, j: (i % tpb, 0)),
            pl.BlockSpec((tm, hd), lambda i, j: (i % tpb, 0)),
        ],
        out_specs=out_spec,
        out_shape=out_shape,
        scratch_shapes=[pltpu.VMEM((tm, k), BF16)],
        compiler_params=_params("parallel", "arbitrary"),
        name=name,
    )(x, w, c_tab, s_tab)


def _out_proj_kernel(y_ref, w_ref, x_ref, g_ref, b_ref, o_ref, *, nsub):
    rs = y_ref.shape[0] // nsub
    for s in range(nsub):
        r = slice(s * rs, (s + 1) * rs)
        acc = _dot(y_ref[r, :], w_ref[...])
        o_ref[r, :] = _layer_norm(DEEPNORM_ALPHA * x_ref[r, :] + acc, g_ref[...], b_ref[...])


def _out_proj(y, w, x, g, b, name):
    m, k = y.shape
    d = w.shape[2]
    tm = 512
    return pl.pallas_call(
        functools.partial(_out_proj_kernel, nsub=2),
        grid=(m // tm,),
        in_specs=[
            pl.BlockSpec((tm, k), lambda i: (i, 0)),
            pl.BlockSpec((None, k, d), lambda i: (0, 0, 0), pipeline_mode=pl.Buffered(1)),
            pl.BlockSpec((tm, d), lambda i: (i, 0)),
            pl.BlockSpec((1, d), lambda i: (0, 0)),
            pl.BlockSpec((1, d), lambda i: (0, 0)),
        ],
        out_specs=pl.BlockSpec((tm, d), lambda i: (i, 0)),
        out_shape=jax.ShapeDtypeStruct((m, d), F32),
        compiler_params=_params("parallel"),
        name=name,
    )(y, w, x, g.reshape(1, d), b.reshape(1, d))


def _retention_kernel(gam_ref, q_ref, k_ref, v_ref, g_ref, dec_ref, xi_ref, zeta_ref, gng_ref, gnb_ref,
                      o_ref, state_ref, *, nh, dk, dv):
    @pl.when(pl.program_id(1) == 0)
    def _():
        state_ref[...] = jnp.zeros_like(state_ref)

    for h in range(nh):
        q = q_ref[:, h * dk:(h + 1) * dk]
        k = k_ref[:, h * dk:(h + 1) * dk]
        v = v_ref[:, h * dv:(h + 1) * dv]
        state = state_ref[h]
        s = _dot_nt(q, k) * dec_ref[h]
        out = _dot(s.astype(BF16), v) + _dot(q, state.astype(BF16)) * xi_ref[h]
        kz = (k.astype(F32) * zeta_ref[h]).astype(BF16)
        state_ref[h] = _dot_tn(kz, v) + gam_ref[h] * state

        mu = jnp.mean(out, axis=-1, keepdims=True)
        d = out - mu
        var = jnp.mean(d * d, axis=-1, keepdims=True)
        cols = slice(h * dv, (h + 1) * dv)
        gn = d * lax.rsqrt(var + LN_EPS) * gng_ref[:, cols] + gnb_ref[:, cols]
        gate = g_ref[:, cols]
        o_ref[:, cols] = ((gate * jax.nn.sigmoid(gate)) * gn).astype(o_ref.dtype)


def _retention(qkv, gate, gn_g, gn_b, bsz, t_len, dk, dv):
    h = RET_HEADS
    c = RET_CHUNK
    nc = t_len // c
    m = bsz * t_len
    log_gamma = jnp.log(1.0 - jnp.power(2.0, -5.0 - jnp.arange(h, dtype=F32)))
    idx = jnp.arange(c, dtype=F32)
    diff = idx[:, None] - idx[None, :]
    decay = jnp.where(diff >= 0, jnp.exp(log_gamma[:, None, None] * jnp.maximum(diff, 0.0)), 0.0)
    xi = jnp.exp(log_gamma[:, None] * (idx + 1.0))[..., None]
    zeta = jnp.exp(log_gamma[:, None] * (c - 1.0 - idx))[..., None]
    gamma_c = jnp.exp(log_gamma * c)
    row = lambda b, cc: b * nc + cc
    whole = lambda shape: pl.BlockSpec(shape, lambda b, cc: (0,) * len(shape))
    return pl.pallas_call(
        functools.partial(_retention_kernel, nh=h, dk=dk, dv=dv),
        grid=(bsz, nc),
        in_specs=[
            pl.BlockSpec(memory_space=pltpu.SMEM),
            pl.BlockSpec((c, h * dk), lambda b, cc: (row(b, cc), 0)),
            pl.BlockSpec((c, h * dk), lambda b, cc: (row(b, cc), 1)),
            pl.BlockSpec((c, h * dv), lambda b, cc: (row(b, cc), 2 * dk // dv)),
            pl.BlockSpec((c, h * dv), lambda b, cc: (row(b, cc), 0)),
            whole((h, c, c)), whole((h, c, 1)), whole((h, c, 1)),
            whole((1, h * dv)), whole((1, h * dv)),
        ],
        out_specs=pl.BlockSpec((c, h * dv), lambda b, cc: (row(b, cc), 0)),
        out_shape=jax.ShapeDtypeStruct((m, h * dv), BF16),
        scratch_shapes=[pltpu.VMEM((h, dk, dv), F32)],
        compiler_params=_params("parallel", "arbitrary"),
        name="retention_chunks",
    )(gamma_c, qkv, qkv, qkv, gate, decay, xi, zeta, gn_g.reshape(1, h * dv), gn_b.reshape(1, h * dv))


def _cmp_mlp_kernel(x_ref, pos_ref, w1a_ref, w1b_ref, b1_ref, w2_ref, c_ref, s_ref, o_ref, *, half):
    x = x_ref[...]
    pos = pos_ref[...]
    a = _dot((x + pos[0:1, :]).astype(BF16), w1a_ref[...])
    bm = _dot((x + pos[1:2, :]).astype(BF16), w1b_ref[...])
    nrows = x.shape[0]
    hid = jax.nn.gelu(a + pltpu.roll(bm, nrows - 1, 0) + b1_ref[...])
    comp = _dot(hid.astype(BF16), w2_ref[...])
    is_key = pl.program_id(0) == 0
    c = jnp.where(is_key, c_ref[...], 1.0)
    s = jnp.where(is_key, s_ref[...], 0.0)
    o_ref[...] = _partial_rotary(comp, c, s, half).astype(o_ref.dtype)


def _cmp_mlp(raw, cmp_pos, w1, b1, w2, c_tab, s_tab, bsz, t_len, half):
    g, hd = NSA_KV_GROUPS, LANES
    nrows = t_len // CMP_STRIDE
    feat = CMP_STRIDE * hd
    hidden = w1.shape[-1]
    x = raw.reshape(2, g, bsz, nrows, feat)
    pos = cmp_pos.reshape(2, CMP_BLOCK // CMP_STRIDE, feat)
    return pl.pallas_call(
        functools.partial(_cmp_mlp_kernel, half=half),
        grid=(2, g, bsz),
        in_specs=[
            pl.BlockSpec((None, None, None, nrows, feat), lambda c, gg, b: (c, gg, b, 0, 0)),
            pl.BlockSpec((None, 2, feat), lambda c, gg, b: (c, 0, 0)),
            pl.BlockSpec((None, feat, hidden), lambda c, gg, b: (c, 0, 0)),
            pl.BlockSpec((None, feat, hidden), lambda c, gg, b: (c, 1, 0)),
            pl.BlockSpec((None, 1, hidden), lambda c, gg, b: (c, 0, 0)),
            pl.BlockSpec((None, hidden, hd), lambda c, gg, b: (c, 0, 0)),
            pl.BlockSpec((nrows, hd), lambda c, gg, b: (0, 0)),
            pl.BlockSpec((nrows, hd), lambda c, gg, b: (0, 0)),
        ],
        out_specs=pl.BlockSpec((None, None, None, nrows, hd), lambda c, gg, b: (c, gg, b, 0, 0)),
        out_shape=jax.ShapeDtypeStruct((2, g, bsz, nrows, hd), BF16),
        compiler_params=_params("arbitrary", "arbitrary", "arbitrary"),
        name="cmp_mlp",
    )(x, pos, w1, w1, b1.reshape(2, 1, hidden), w2, c_tab, s_tab)


def _with_ones(v):
    return jnp.concatenate([v, jnp.ones_like(v)], axis=1)


def _nsa_kernel(q_ref, gate_ref, kc_ref, vc_ref, ks_ref, vs_ref, kw_ref, vw_ref, ovt_ref, ex_ref,
                o_ref, mask_ref, *, tq, tc, hg, hd, max_chunks):
    qi = pl.program_id(2)
    t0 = qi * tq
    rows = hg * tq
    q = q_ref[...]
    q4 = jnp.concatenate([q[:, i * hd:(i + 1) * hd] for i in range(hg)], axis=0)
    tpos = t0 + lax.broadcasted_iota(jnp.int32, (tq, 1), 0)

    ncmp = kc_ref.shape[0]
    s3 = _dot_nt(q4, kc_ref[...]).reshape(hg, tq, ncmp)
    cmp_end = lax.broadcasted_iota(jnp.int32, (tq, ncmp), 1) * CMP_STRIDE + (CMP_BLOCK - 1)
    s3 = jnp.where((cmp_end <= tpos)[None], s3, NEG_INF)
    e3 = jnp.exp2(s3 - jnp.max(s3, axis=-1, keepdims=True))
    has_cmp = (tpos >= CMP_BLOCK - 1).astype(F32)
    p3 = (e3 / jnp.sum(e3, axis=-1, keepdims=True) * has_cmp[None]).astype(BF16)
    o_cmp = _dot(p3.reshape(rows, ncmp), vc_ref[...])

    nslc = ovt_ref.shape[0]
    p_slc = _dot_nt(ovt_ref[...], jnp.concatenate([p3[i] for i in range(hg)], axis=1))
    trow = t0 + lax.broadcasted_iota(jnp.int32, (1, tq), 1)
    blk = lax.broadcasted_iota(jnp.int32, (nslc, tq), 0)
    cur = jnp.right_shift(trow, SLC_BLOCK.bit_length() - 1)
    valid = blk * SLC_BLOCK <= trow
    forced = (blk == 0) | (blk == cur) | (blk == cur - 1)
    score = jnp.where(valid, p_slc + jnp.where(forced, FORCE_BONUS, 0.0), -1.0)
    rank = jnp.zeros((nslc, tq), jnp.int32)
    for kb in range(nslc):
        ck = score[kb:kb + 1, :]
        rank += ((ck > score) | ((ck == score) & (blk > kb))).astype(jnp.int32)
    sel = jnp.where(rank < SLC_TOP, 1.0, 0.0).astype(BF16)
    mask_ref[...] = _dot_tn(sel, ex_ref[...])

    gts = gate_ref[...]

    def window_and_selected(n_chunks):
        wlen = WINDOW + tq
        w0 = pl.multiple_of(jnp.maximum(qi - WINDOW // tq, 0) * tq, tq)
        sw = _dot_nt(q4, kw_ref[pl.ds(w0, wlen), :]).reshape(hg, tq, wlen)
        kpos = w0 + lax.broadcasted_iota(jnp.int32, (tq, wlen), 1)
        okw = (kpos <= tpos) & (kpos > tpos - WINDOW)
        sw = jnp.where(okw[None], sw, NEG_INF)
        ew = jnp.exp2(sw - jnp.max(sw, axis=-1, keepdims=True)).astype(BF16)
        rw = _dot(ew.reshape(rows, wlen), _with_ones(vw_ref[pl.ds(w0, wlen), :]))
        o_win = rw[:, :hd] / rw[:, hd:]

        m_i = acc = None
        for c in range(n_chunks):
            k0 = c * tc
            s = _dot_nt(q4, ks_ref[k0:k0 + tc, :]).reshape(hg, tq, tc)
            ok = mask_ref[:, k0:k0 + tc] > 0.5
            if c == n_chunks - 1:
                ok &= (k0 + lax.broadcasted_iota(jnp.int32, (tq, tc), 1)) <= tpos
            s = jnp.where(ok[None], s, NEG_INF)
            m_c = jnp.max(s, axis=-1, keepdims=True)
            m_new = m_c if c == 0 else jnp.maximum(m_i, m_c)
            pv = _dot(jnp.exp2(s - m_new).astype(BF16).reshape(rows, tc), _with_ones(vs_ref[k0:k0 + tc, :]))
            acc = pv if c == 0 else acc * jnp.exp2(m_i - m_new).reshape(rows, 1) + pv
            m_i = m_new
        o_slc = acc[:, :hd] / acc[:, hd:]

        for i in range(hg):
            r = slice(i * tq, (i + 1) * tq)
            o = (gts[:, 3 * i:3 * i + 1] * o_cmp[r] + gts[:, 3 * i + 1:3 * i + 2] * o_slc[r]
                 + gts[:, 3 * i + 2:3 * i + 3] * o_win[r])
            o_ref[:, i * hd:(i + 1) * hd] = o.astype(o_ref.dtype)

    n_chunks = lax.div(t0, tc) + 1
    for n in range(1, max_chunks + 1):
        pl.when(n_chunks == n)(functools.partial(window_and_selected, n))


def _nsa_attention(q, gates, kv_cmp, kv_rest, bsz, t_len):
    g, hg, hd = NSA_KV_GROUPS, NSA_HEADS_PER_GROUP, LANES
    tq, tc = NSA_Q_TILE, NSA_KEY_CHUNK
    nq = t_len // tq
    ncmp_pad = t_len // CMP_STRIDE
    n_cmp = (t_len - CMP_BLOCK) // CMP_STRIDE + 1
    n_slc = t_len // SLC_BLOCK
    cs = np.arange(ncmp_pad)[None, :] * CMP_STRIDE
    ss = np.arange(n_slc)[:, None] * SLC_BLOCK
    ovt = ((cs < ss + SLC_BLOCK) & (cs + CMP_BLOCK > ss) & (np.arange(ncmp_pad)[None, :] < n_cmp)).astype(np.float32)
    ovt = jnp.asarray(np.tile(ovt, (1, hg)), BF16)
    expand = jnp.asarray((np.arange(t_len)[None, :] // SLC_BLOCK == np.arange(n_slc)[:, None]).astype(np.float32), BF16)
    kv_spec = lambda slot: pl.BlockSpec((None, None, t_len, hd), lambda b, gg, qi: (slot, gg, b, 0))
    cmp_spec = lambda slot: pl.BlockSpec((None, None, None, ncmp_pad, hd), lambda b, gg, qi: (slot, gg, b, 0, 0))
    return pl.pallas_call(
        functools.partial(_nsa_kernel, tq=tq, tc=tc, hg=hg, hd=hd, max_chunks=t_len // tc),
        grid=(bsz, g, nq),
        in_specs=[
            pl.BlockSpec((tq, hg * hd), lambda b, gg, qi: (b * nq + qi, gg)),
            pl.BlockSpec((tq, LANES), lambda b, gg, qi: (b * nq + qi, gg)),
            cmp_spec(0), cmp_spec(1),
            kv_spec(0), kv_spec(1), kv_spec(2), kv_spec(3),
            pl.BlockSpec((n_slc, hg * ncmp_pad), lambda b, gg, qi: (0, 0)),
            pl.BlockSpec((n_slc, t_len), lambda b, gg, qi: (0, 0)),
        ],
        out_specs=pl.BlockSpec((tq, hg * hd), lambda b, gg, qi: (b * nq + qi, gg)),
        out_shape=jax.ShapeDtypeStruct((bsz * t_len, g * hg * hd), BF16),
        scratch_shapes=[pltpu.VMEM((tq, t_len), F32)],
        compiler_params=_params("parallel", "parallel", "arbitrary"),
        name="nsa_attention",
    )(q, gates, kv_cmp, kv_cmp, kv_rest, kv_rest, kv_rest, kv_rest, ovt, expand)


def _rope_tables(pos, rot_dims, theta):
    half = rot_dims // 2
    inv_freq = jnp.power(jnp.float32(theta), -jnp.arange(half, dtype=F32) / half)
    ang = pos.astype(F32)[:, None] * inv_freq[None, :]
    cos, sin = jnp.cos(ang), jnp.sin(ang)
    n = pos.shape[0]
    pad = LANES - rot_dims
    c_tab = jnp.concatenate([cos, cos, jnp.ones((n, pad), F32)], axis=1)
    s_tab = jnp.concatenate([-sin, sin, jnp.zeros((n, pad), F32)], axis=1)
    return c_tab, s_tab


def kernel(x, ln_g, ln_b, ffn1_w_in, ffn1_w_out, ffn2_w_in, ffn2_w_out, ret_w_in, ret_gn_g, ret_gn_b, ret_w_out, kv_w, cmp_pos, cmp_w1, cmp_b1, cmp_w2, nsa_w_q, nsa_w_out):
    bsz, t_len, d = x.shape
    m = bsz * t_len
    bf = lambda w: w.astype(BF16)
    h = x.reshape(m, d)
    pos = jnp.arange(t_len)

    h, w_in_next, w_out_next = _ffn(h, bf(ffn1_w_in[0]), bf(ffn1_w_out[0]), ln_g[0, 0], ln_b[0, 0],
                                    cast_next=(ffn2_w_in, ffn2_w_out, 0))
    dk = d // RET_HEADS
    dv = 2 * d // RET_HEADS
    half = dk // 2
    inv_freq = jnp.power(jnp.float32(RET_ROPE_BASE), -jnp.arange(half, dtype=F32) / half)
    ang = pos.astype(F32)[:, None] * inv_freq[None, :]
    w_ret = bf(ret_w_in)
    n_qkv = 2 * RET_HEADS * dk + RET_HEADS * dv
    qkv = _ret_qkv(h, w_ret, jnp.cos(ang), jnp.sin(ang), t_len, dk, n_qkv)
    gate = _proj(h, w_ret, n_qkv, RET_HEADS * dv, F32, name="ret_gate_proj")
    y = _retention(qkv, gate, ret_gn_g[0], ret_gn_b[0], bsz, t_len, dk, dv)
    h = _out_proj(y, bf(ret_w_out), h, ln_g[0, 1], ln_b[0, 1], "ret_out_proj")
    h, w_in_next, w_out_next = _ffn(h, w_in_next, w_out_next, ln_g[0, 2], ln_b[0, 2],
                                    cast_next=(ffn1_w_in, ffn1_w_out, 1))

    hd = d // NSA_HEADS
    rope_dims = hd // 4
    rope_half = rope_dims // 2
    c_tab, s_tab = _rope_tables(pos, rope_dims, ROPE_THETA)
    w_kv = bf(kv_w)[None]
    slot_cols = NSA_KV_GROUPS * hd
    raw_cmp = _head_proj(h, w_kv, c_tab, s_tab, t_len, col0=0, nslots=2, out_dtype=F32, half=rope_half,
                         scale=1.0, rotate="none", split=True, name="kv_cmp_proj")
    kv_rest = _head_proj(h, w_kv, c_tab, s_tab, t_len, col0=2 * slot_cols, nslots=4, out_dtype=BF16,
                         half=rope_half, scale=1.0, rotate="even", split=True, name="kv_slc_win_proj")
    n_rows = t_len // CMP_STRIDE
    cmp_end = jnp.arange(n_rows) * CMP_STRIDE + CMP_BLOCK - 1
    cc_tab, cs_tab = _rope_tables(cmp_end, rope_dims, ROPE_THETA)
    kv_cmp = _cmp_mlp(raw_cmp, cmp_pos, bf(cmp_w1), cmp_b1, bf(cmp_w2), cc_tab, cs_tab, bsz, t_len, rope_half)

    h, w_in_next, w_out_next = _ffn(h, w_in_next, w_out_next, ln_g[1, 0], ln_b[1, 0],
                                    cast_next=(ffn2_w_in, ffn2_w_out, 1))
    n_q = NSA_HEADS * hd
    q = _head_proj(h, bf(nsa_w_q), c_tab, s_tab, t_len, col0=0, nslots=n_q // 512, out_dtype=BF16,
                   half=rope_half, scale=hd ** -0.5 * LOG2_E, rotate="all", split=False, name="nsa_q_proj")
    w_gate = nsa_w_q[0][:, n_q:].reshape(d, NSA_KV_GROUPS, NSA_HEADS_PER_GROUP * 3)
    w_gate = jnp.pad(w_gate, ((0, 0), (0, 0), (0, LANES - NSA_HEADS_PER_GROUP * 3))).reshape(1, d, NSA_KV_GROUPS * LANES)
    gates = _proj(h, bf(w_gate), 0, NSA_KV_GROUPS * LANES, F32, act="sigmoid", name="nsa_gate_proj")
    o = _nsa_attention(q, gates, kv_cmp, kv_rest, bsz, t_len)
    h = _out_proj(o, bf(nsa_w_out), h, ln_g[1, 1], ln_b[1, 1], "nsa_out_proj")
    h = _ffn(h, w_in_next, w_out_next, ln_g[1, 2], ln_b[1, 2])
    return h.reshape(bsz, t_len, d)
```

```python
import functools
import math

import numpy as np
import jax
import jax.numpy as jnp
from jax import lax
from jax.experimental import pallas as pl
from jax.experimental.pallas import tpu as pltpu

F32 = jnp.float32
BF16 = jnp.bfloat16

DEPTH = 2
N_A_LAYERS = DEPTH // 2
DEEPNORM_ALPHA = (2 * DEPTH) ** 0.25
LN_EPS = 1e-5
MACARON_WEIGHT = 0.5

RET_HEADS = 8
RET_CHUNK = 256
RET_ROPE_BASE = 10000.0

NSA_HEADS = 16
NSA_KV_GROUPS = 4
NSA_HEADS_PER_GROUP = NSA_HEADS // NSA_KV_GROUPS
CMP_BLOCK = 32
CMP_STRIDE = 16
SLC_BLOCK = 64
SLC_TOP = 8
WINDOW = 512
FORCE_BONUS = 1e4
ROPE_THETA = 500000.0
NEG_INF = -1e30
LOG2_E = math.log2(math.e)

V7X_VMEM_BYTES = 64 * 1024 * 1024
VMEM_LIMIT = V7X_VMEM_BYTES - 8 * 1024 * 1024
LANES = 128

FFN_SUBTILES = 2
NSA_Q_TILE = 256
NSA_KEY_CHUNK = 512


def _params(*semantics):
    return pltpu.CompilerParams(dimension_semantics=semantics, vmem_limit_bytes=VMEM_LIMIT)


def _layer_norm(y, g, b):
    mu = jnp.mean(y, axis=-1, keepdims=True)
    d = y - mu
    var = jnp.mean(d * d, axis=-1, keepdims=True)
    return d * lax.rsqrt(var + LN_EPS) * g + b


def _dot(a, b):
    return jnp.dot(a, b, preferred_element_type=F32)


def _dot_nt(a, b):
    return lax.dot_general(a, b, (((1,), (1,)), ((), ())), preferred_element_type=F32)


def _dot_tn(a, b):
    return lax.dot_general(a, b, (((0,), (0,)), ((), ())), preferred_element_type=F32)


def _ffn_kernel(*refs, nf, has_xb, cast_next, emit_bf16):
    refs = list(refs)
    x_ref = refs.pop(0)
    xb_ref = refs.pop(0) if has_xb else refs.pop()
    wa_ref, wu_ref, wo_ref, g_ref, b_ref = refs[:5]
    refs = refs[5:]
    if cast_next:
        nwi_ref, nwo_ref = refs[:2]
        refs = refs[2:]
    o_ref = refs.pop(0)
    ob_ref = refs.pop(0) if emit_bf16 else None
    f = pl.program_id(1)
    if cast_next:
        nwi_out, nwo_out = refs
        ci, ro = nwi_ref.shape[1] // nf, nwo_ref.shape[0] // nf
        cols = pl.ds(pl.multiple_of(f * ci, LANES), ci)
        rows = pl.ds(pl.multiple_of(f * ro, 16), ro)
        nwi_out[:, cols] = nwi_ref[:, cols].astype(BF16)
        nwo_out[rows, :] = nwo_ref[rows, :].astype(BF16)

    @pl.when(f == 0)
    def _():
        if not has_xb:
            xb_ref[...] = x_ref[...].astype(BF16)
        o_ref[...] = jnp.zeros_like(o_ref)

    xb = xb_ref[...]
    tf = wa_ref.shape[1]
    sub = tf // FFN_SUBTILES
    part = None
    for s in range(FFN_SUBTILES):
        a = _dot(xb, wa_ref[:, s * sub:(s + 1) * sub])
        u = _dot(xb, wu_ref[:, s * sub:(s + 1) * sub])
        h = (a * jax.nn.sigmoid(a)) * u
        p = _dot(h.astype(BF16), wo_ref[s * sub:(s + 1) * sub, :])
        part = p if part is None else part + p
    o_ref[...] += part

    @pl.when(f == nf - 1)
    def _():
        y = DEEPNORM_ALPHA * x_ref[...] + MACARON_WEIGHT * o_ref[...]
        out = _layer_norm(y, g_ref[...], b_ref[...])
        o_ref[...] = out
        if emit_bf16:
            ob_ref[...] = out.astype(BF16)


def _ffn(x, xb, w_in, w_out, g, b, cast_next=None, emit_bf16=True):
    m, d = x.shape
    f_dim = w_out.shape[0]
    tm, tf = 512, 512
    ni, nf = m // tm, f_dim // tf
    row_spec = pl.BlockSpec((tm, d), lambda i, f: (i, 0))
    vec_spec = pl.BlockSpec((1, d), lambda i, f: (0, 0))
    in_specs, args = [row_spec], [x]
    if xb is not None:
        in_specs.append(row_spec)
        args.append(xb)
    in_specs += [pl.BlockSpec((d, tf), lambda i, f: (0, f)), pl.BlockSpec((d, tf), lambda i, f: (0, f + nf)),
                 pl.BlockSpec((tf, d), lambda i, f: (f, 0)), vec_spec, vec_spec]
    args += [w_in, w_in, w_out, g.reshape(1, d), b.reshape(1, d)]
    out_specs, out_shape = [row_spec], [jax.ShapeDtypeStruct((m, d), F32)]
    if emit_bf16:
        out_specs.append(row_spec)
        out_shape.append(jax.ShapeDtypeStruct((m, d), BF16))
    if cast_next is not None:
        nw_in, nw_out, layer = cast_next
        ri, ro = d // ni, f_dim // ni
        assert ri % 16 == 0 and (2 * f_dim // nf) % LANES == 0 and (ro // nf) % 16 == 0 and ro % nf == 0
        in_specs += [pl.BlockSpec((None, ri, 2 * f_dim), lambda i, f: (layer, i, 0)),
                     pl.BlockSpec((None, ro, d), lambda i, f: (layer, i, 0))]
        out_specs += [pl.BlockSpec((ri, 2 * f_dim), lambda i, f: (i, 0)), pl.BlockSpec((ro, d), lambda i, f: (i, 0))]
        out_shape += [jax.ShapeDtypeStruct((d, 2 * f_dim), BF16), jax.ShapeDtypeStruct((f_dim, d), BF16)]
        args += [nw_in, nw_out]
    return pl.pallas_call(
        functools.partial(_ffn_kernel, nf=nf, has_xb=xb is not None, cast_next=cast_next is not None,
                          emit_bf16=emit_bf16),
        grid=(ni, nf),
        in_specs=in_specs,
        out_specs=out_specs,
        out_shape=out_shape,
        scratch_shapes=[] if xb is not None else [pltpu.VMEM((tm, d), BF16)],
        compiler_params=_params("arbitrary", "arbitrary"),
        name="ffn_deepnorm",
    )(*args)


def _wproj_kernel(*refs, n_extra, epilogue):
    xb_ref, w_ref = refs[:2]
    extra = refs[2:2 + n_extra]
    o_ref, wb_ref = refs[2 + n_extra:]

    @pl.when(pl.program_id(1) == 0)
    def _():
        wb_ref[...] = w_ref[...].astype(BF16)

    epilogue(_dot(xb_ref[...], wb_ref[...]), pl.program_id(0), extra, o_ref)


def _wproj(xb, w, *, col0, ncols, epilogue, out_spec, out_shape, extra=(), extra_specs=(), tm=1024, name):
    m, k = xb.shape
    tn = min(1024, ncols)
    assert col0 % tn == 0 and ncols % tn == 0
    return pl.pallas_call(
        functools.partial(_wproj_kernel, n_extra=len(extra), epilogue=epilogue),
        grid=(ncols // tn, m // tm),
        in_specs=[pl.BlockSpec((tm, k), lambda j, i: (i, 0)),
                  pl.BlockSpec((None, k, tn), lambda j, i: (0, 0, j + col0 // tn)), *extra_specs],
        out_specs=out_spec(tm, tn),
        out_shape=out_shape,
        scratch_shapes=[pltpu.VMEM((k, tn), BF16)],
        compiler_params=_params("arbitrary", "arbitrary"),
        name=name,
    )(xb, w, *extra)


def _table_specs(t_len, width, tm):
    tpb = t_len // tm
    return [pl.BlockSpec((tm, width), lambda j, i: (i % tpb, 0))] * 2


def _ret_epilogue(acc, j, extra, o_ref, *, q_tiles, k_scale, half):
    cos_ref, sin_ref = extra
    tn = acc.shape[1]

    @pl.when(j < 2 * q_tiles)
    def _():
        scale = jnp.where(j >= q_tiles, k_scale, 1.0).astype(F32)
        cos = cos_ref[...]
        sin = sin_ref[...]
        for h in range(tn // (2 * half)):
            c0 = h * 2 * half
            x1 = acc[:, c0:c0 + half]
            x2 = acc[:, c0 + half:c0 + 2 * half]
            o_ref[:, c0:c0 + half] = ((x1 * cos - x2 * sin) * scale).astype(o_ref.dtype)
            o_ref[:, c0 + half:c0 + 2 * half] = ((x2 * cos + x1 * sin) * scale).astype(o_ref.dtype)

    @pl.when(j >= 2 * q_tiles)
    def _():
        o_ref[...] = acc.astype(o_ref.dtype)


def _partial_rotary(x, c, s, half):
    lane = lax.broadcasted_iota(jnp.int32, x.shape, 1)
    partner = jnp.where(lane < half, pltpu.roll(x, LANES - half, 1), pltpu.roll(x, half, 1))
    return x * c + partner * s


def _heads_epilogue(acc, j, extra, o_ref, *, hd, hps, half, scale, rotate, split):
    c_ref, s_ref = extra
    for h in range(acc.shape[1] // hd):
        piece = acc[:, h * hd:(h + 1) * hd]
        if rotate == "all" or (rotate == "even" and h < hps):
            piece = _partial_rotary(piece, c_ref[...], s_ref[...], half)
        if scale != 1.0:
            piece = piece * scale
        if split:
            o_ref[h // hps, h % hps] = piece.astype(o_ref.dtype)
        else:
            o_ref[:, h * hd:(h + 1) * hd] = piece.astype(o_ref.dtype)


def _sigmoid_epilogue(acc, j, extra, o_ref):
    o_ref[...] = jax.nn.sigmoid(acc).astype(o_ref.dtype)


def _head_proj(xb, w, c_tab, s_tab, t_len, *, col0, nslots, out_dtype, half, scale, rotate, split, name):
    m = xb.shape[0]
    hd, hps = LANES, NSA_KV_GROUPS
    assert nslots % 2 == 0
    tm = 1024
    if split:
        out_spec = lambda tm, tn: pl.BlockSpec((2, hps, tm, hd), lambda j, i: (j, 0, i, 0))
        out_shape = jax.ShapeDtypeStruct((nslots, hps, m, hd), out_dtype)
    else:
        out_spec = lambda tm, tn: pl.BlockSpec((tm, tn), lambda j, i: (i, j))
        out_shape = jax.ShapeDtypeStruct((m, nslots * hps * hd), out_dtype)
    epi = functools.partial(_heads_epilogue, hd=hd, hps=hps, half=half, scale=scale, rotate=rotate, split=split)
    return _wproj(xb, w, col0=col0, ncols=nslots * hps * hd, epilogue=epi, out_spec=out_spec, out_shape=out_shape,
                  extra=(c_tab, s_tab), extra_specs=_table_specs(t_len, hd, tm), tm=tm, name=name)


def _out_proj_kernel(y_ref, w_ref, x_ref, g_ref, b_ref, o_ref, *, nsub):
    rs = y_ref.shape[0] // nsub
    for s in range(nsub):
        r = slice(s * rs, (s + 1) * rs)
        acc = _dot(y_ref[r, :], w_ref[...])
        o_ref[r, :] = _layer_norm(DEEPNORM_ALPHA * x_ref[r, :] + acc, g_ref[...], b_ref[...])


def _out_proj(y, w, x, g, b, name):
    m, k = y.shape
    d = w.shape[2]
    tm = 512
    row_spec = pl.BlockSpec((tm, d), lambda i: (i, 0))
    return pl.pallas_call(
        functools.partial(_out_proj_kernel, nsub=2),
        grid=(m // tm,),
        in_specs=[
            pl.BlockSpec((tm, k), lambda i: (i, 0)),
            pl.BlockSpec((None, k, d), lambda i: (0, 0, 0), pipeline_mode=pl.Buffered(1)),
            row_spec,
            pl.BlockSpec((1, d), lambda i: (0, 0)),
            pl.BlockSpec((1, d), lambda i: (0, 0)),
        ],
        out_specs=row_spec,
        out_shape=jax.ShapeDtypeStruct((m, d), F32),
        compiler_params=_params("parallel"),
        name=name,
    )(y, w, x, g.reshape(1, d), b.reshape(1, d))


def _retention_kernel(gam_ref, q_ref, k_ref, v_ref, g_ref, dec_ref, xi_ref, zeta_ref, gng_ref, gnb_ref,
                      o_ref, state_ref, *, nh, dk, dv):
    @pl.when(pl.program_id(1) == 0)
    def _():
        state_ref[...] = jnp.zeros_like(state_ref)

    for h in range(nh):
        q = q_ref[:, h * dk:(h + 1) * dk]
        k = k_ref[:, h * dk:(h + 1) * dk]
        v = v_ref[:, h * dv:(h + 1) * dv]
        state = state_ref[h]
        s = _dot_nt(q, k) * dec_ref[h]
        out = _dot(s.astype(BF16), v) + _dot(q, state.astype(BF16)) * xi_ref[h]
        kz = (k.astype(F32) * zeta_ref[h]).astype(BF16)
        state_ref[h] = _dot_tn(kz, v) + gam_ref[h] * state

        mu = jnp.mean(out, axis=-1, keepdims=True)
        d = out - mu
        var = jnp.mean(d * d, axis=-1, keepdims=True)
        cols = slice(h * dv, (h + 1) * dv)
        gn = d * lax.rsqrt(var + LN_EPS) * gng_ref[:, cols] + gnb_ref[:, cols]
        gate = g_ref[:, cols].astype(F32)
        o_ref[:, cols] = ((gate * jax.nn.sigmoid(gate)) * gn).astype(o_ref.dtype)


def _retention(proj, gn_g, gn_b, bsz, t_len, dk, dv):
    h = RET_HEADS
    c = RET_CHUNK
    nc = t_len // c
    m = bsz * t_len
    log_gamma = jnp.log(1.0 - jnp.power(2.0, -5.0 - jnp.arange(h, dtype=F32)))
    idx = jnp.arange(c, dtype=F32)
    diff = idx[:, None] - idx[None, :]
    decay = jnp.where(diff >= 0, jnp.exp(log_gamma[:, None, None] * jnp.maximum(diff, 0.0)), 0.0)
    xi = jnp.exp(log_gamma[:, None] * (idx + 1.0))[..., None]
    zeta = jnp.exp(log_gamma[:, None] * (c - 1.0 - idx))[..., None]
    gamma_c = jnp.exp(log_gamma * c)
    row = lambda b, cc: b * nc + cc
    whole = lambda shape: pl.BlockSpec(shape, lambda b, cc: (0,) * len(shape))
    v_blk = 2 * dk // dv
    return pl.pallas_call(
        functools.partial(_retention_kernel, nh=h, dk=dk, dv=dv),
        grid=(bsz, nc),
        in_specs=[
            pl.BlockSpec(memory_space=pltpu.SMEM),
            pl.BlockSpec((c, h * dk), lambda b, cc: (row(b, cc), 0)),
            pl.BlockSpec((c, h * dk), lambda b, cc: (row(b, cc), 1)),
            pl.BlockSpec((c, h * dv), lambda b, cc: (row(b, cc), v_blk)),
            pl.BlockSpec((c, h * dv), lambda b, cc: (row(b, cc), v_blk + 1)),
            whole((h, c, c)), whole((h, c, 1)), whole((h, c, 1)),
            whole((1, h * dv)), whole((1, h * dv)),
        ],
        out_specs=pl.BlockSpec((c, h * dv), lambda b, cc: (row(b, cc), 0)),
        out_shape=jax.ShapeDtypeStruct((m, h * dv), BF16),
        scratch_shapes=[pltpu.VMEM((h, dk, dv), F32)],
        compiler_params=_params("parallel", "arbitrary"),
        name="retention_chunks",
    )(gamma_c, proj, proj, proj, proj, decay, xi, zeta, gn_g.reshape(1, h * dv), gn_b.reshape(1, h * dv))


def _cmp_mlp_kernel(x_ref, pos_ref, w1a_ref, w1b_ref, b1_ref, w2_ref, c_ref, s_ref, o_ref, *, half):
    x = x_ref[...]
    pos = pos_ref[...]
    a = _dot((x + pos[0:1, :]).astype(BF16), w1a_ref[...])
    bm = _dot((x + pos[1:2, :]).astype(BF16), w1b_ref[...])
    nrows = x.shape[0]
    hid = jax.nn.gelu(a + pltpu.roll(bm, nrows - 1, 0) + b1_ref[...])
    comp = _dot(hid.astype(BF16), w2_ref[...])
    is_key = pl.program_id(0) == 0
    c = jnp.where(is_key, c_ref[...], 1.0)
    s = jnp.where(is_key, s_ref[...], 0.0)
    o_ref[...] = _partial_rotary(comp, c, s, half).astype(o_ref.dtype)


def _cmp_mlp(raw, cmp_pos, w1, b1, w2, c_tab, s_tab, bsz, t_len, half):
    g, hd = NSA_KV_GROUPS, LANES
    nrows = t_len // CMP_STRIDE
    feat = CMP_STRIDE * hd
    hidden = w1.shape[-1]
    x = raw.reshape(2, g, bsz, nrows, feat)
    pos = cmp_pos.reshape(2, CMP_BLOCK // CMP_STRIDE, feat)
    return pl.pallas_call(
        functools.partial(_cmp_mlp_kernel, half=half),
        grid=(2, g, bsz),
        in_specs=[
            pl.BlockSpec((None, None, None, nrows, feat), lambda c, gg, b: (c, gg, b, 0, 0)),
            pl.BlockSpec((None, 2, feat), lambda c, gg, b: (c, 0, 0)),
            pl.BlockSpec((None, feat, hidden), lambda c, gg, b: (c, 0, 0)),
            pl.BlockSpec((None, feat, hidden), lambda c, gg, b: (c, 1, 0)),
            pl.BlockSpec((None, 1, hidden), lambda c, gg, b: (c, 0, 0)),
            pl.BlockSpec((None, hidden, hd), lambda c, gg, b: (c, 0, 0)),
            pl.BlockSpec((nrows, hd), lambda c, gg, b: (0, 0)),
            pl.BlockSpec((nrows, hd), lambda c, gg, b: (0, 0)),
        ],
        out_specs=pl.BlockSpec((None, None, None, nrows, hd), lambda c, gg, b: (c, gg, b, 0, 0)),
        out_shape=jax.ShapeDtypeStruct((2, g, bsz, nrows, hd), BF16),
        compiler_params=_params("arbitrary", "arbitrary", "arbitrary"),
        name="cmp_mlp",
    )(x, pos, w1, w1, b1.reshape(2, 1, hidden), w2, c_tab, s_tab)


def _with_ones(v):
    return jnp.concatenate([v, jnp.ones_like(v)], axis=1)


def _nsa_kernel(q_ref, gate_ref, kc_ref, vc_ref, ks_ref, vs_ref, kw_ref, vw_ref, ovt_ref, ex_ref,
                o_ref, mask_ref, *, tq, tc, hg, hd, max_chunks):
    qi = pl.program_id(2)
    t0 = qi * tq
    rows = hg * tq
    q = q_ref[...]
    q4 = jnp.concatenate([q[:, i * hd:(i + 1) * hd] for i in range(hg)], axis=0)
    tpos = t0 + lax.broadcasted_iota(jnp.int32, (tq, 1), 0)

    ncmp = kc_ref.shape[0]
    s3 = _dot_nt(q4, kc_ref[...]).reshape(hg, tq, ncmp)
    cmp_end = lax.broadcasted_iota(jnp.int32, (tq, ncmp), 1) * CMP_STRIDE + (CMP_BLOCK - 1)
    s3 = jnp.where((cmp_end <= tpos)[None], s3, NEG_INF)
    e3 = jnp.exp2(s3 - jnp.max(s3, axis=-1, keepdims=True))
    has_cmp = (tpos >= CMP_BLOCK - 1).astype(F32)
    p3 = (e3 / jnp.sum(e3, axis=-1, keepdims=True) * has_cmp[None]).astype(BF16)
    o_cmp = _dot(p3.reshape(rows, ncmp), vc_ref[...])

    nslc = ovt_ref.shape[0]
    p_slc = _dot_nt(ovt_ref[...], jnp.concatenate([p3[i] for i in range(hg)], axis=1))
    trow = t0 + lax.broadcasted_iota(jnp.int32, (1, tq), 1)
    blk = lax.broadcasted_iota(jnp.int32, (nslc, tq), 0)
    cur = jnp.right_shift(trow, SLC_BLOCK.bit_length() - 1)
    valid = blk * SLC_BLOCK <= trow
    forced = (blk == 0) | (blk == cur) | (blk == cur - 1)
    score = jnp.where(valid, p_slc + jnp.where(forced, FORCE_BONUS, 0.0), -1.0)
    rank = jnp.zeros((nslc, tq), jnp.int32)
    for kb in range(nslc):
        ck = score[kb:kb + 1, :]
        rank += ((ck > score) | ((ck == score) & (blk > kb))).astype(jnp.int32)
    sel = jnp.where(rank < SLC_TOP, 1.0, 0.0).astype(BF16)
    mask_ref[...] = _dot_tn(sel, ex_ref[...])

    gts = gate_ref[...]

    def window_and_selected(n_chunks):
        wlen = WINDOW + tq
        w0 = pl.multiple_of(jnp.maximum(qi - WINDOW // tq, 0) * tq, tq)
        sw = _dot_nt(q4, kw_ref[pl.ds(w0, wlen), :]).reshape(hg, tq, wlen)
        kpos = w0 + lax.broadcasted_iota(jnp.int32, (tq, wlen), 1)
        okw = (kpos <= tpos) & (kpos > tpos - WINDOW)
        sw = jnp.where(okw[None], sw, NEG_INF)
        ew = jnp.exp2(sw - jnp.max(sw, axis=-1, keepdims=True)).astype(BF16)
        rw = _dot(ew.reshape(rows, wlen), _with_ones(vw_ref[pl.ds(w0, wlen), :]))
        o_win = rw[:, :hd] / rw[:, hd:]

        m_i = acc = None
        for c in range(n_chunks):
            k0 = c * tc
            s = _dot_nt(q4, ks_ref[k0:k0 + tc, :]).reshape(hg, tq, tc)
            ok = mask_ref[:, k0:k0 + tc] > 0.5
            if c == n_chunks - 1:
                ok &= (k0 + lax.broadcasted_iota(jnp.int32, (tq, tc), 1)) <= tpos
            s = jnp.where(ok[None], s, NEG_INF)
            m_c = jnp.max(s, axis=-1, keepdims=True)
            m_new = m_c if c == 0 else jnp.maximum(m_i, m_c)
            pv = _dot(jnp.exp2(s - m_new).astype(BF16).reshape(rows, tc), _with_ones(vs_ref[k0:k0 + tc, :]))
            acc = pv if c == 0 else acc * jnp.exp2(m_i - m_new).reshape(rows, 1) + pv
            m_i = m_new
        o_slc = acc[:, :hd] / acc[:, hd:]

        for i in range(hg):
            r = slice(i * tq, (i + 1) * tq)
            o = (gts[:, 3 * i:3 * i + 1] * o_cmp[r] + gts[:, 3 * i + 1:3 * i + 2] * o_slc[r]
                 + gts[:, 3 * i + 2:3 * i + 3] * o_win[r])
            o_ref[:, i * hd:(i + 1) * hd] = o.astype(o_ref.dtype)

    n_chunks = lax.div(t0, tc) + 1
    for n in range(1, max_chunks + 1):
        pl.when(n_chunks == n)(functools.partial(window_and_selected, n))


def _nsa_attention(q, gates, kv_cmp, kv_rest, bsz, t_len):
    g, hg, hd = NSA_KV_GROUPS, NSA_HEADS_PER_GROUP, LANES
    tq, tc = NSA_Q_TILE, NSA_KEY_CHUNK
    nq = t_len // tq
    ncmp_pad = t_len // CMP_STRIDE
    n_cmp = (t_len - CMP_BLOCK) // CMP_STRIDE + 1
    n_slc = t_len // SLC_BLOCK
    cs = np.arange(ncmp_pad)[None, :] * CMP_STRIDE
    ss = np.arange(n_slc)[:, None] * SLC_BLOCK
    ovt = ((cs < ss + SLC_BLOCK) & (cs + CMP_BLOCK > ss) & (np.arange(ncmp_pad)[None, :] < n_cmp)).astype(np.float32)
    ovt = jnp.asarray(np.tile(ovt, (1, hg)), BF16)
    expand = jnp.asarray((np.arange(t_len)[None, :] // SLC_BLOCK == np.arange(n_slc)[:, None]).astype(np.float32), BF16)
    kv_spec = lambda slot: pl.BlockSpec((None, None, t_len, hd), lambda b, gg, qi: (slot, gg, b, 0))
    cmp_spec = lambda slot: pl.BlockSpec((None, None, None, ncmp_pad, hd), lambda b, gg, qi: (slot, gg, b, 0, 0))
    return pl.pallas_call(
        functools.partial(_nsa_kernel, tq=tq, tc=tc, hg=hg, hd=hd, max_chunks=t_len // tc),
        grid=(bsz, g, nq),
        in_specs=[
            pl.BlockSpec((tq, hg * hd), lambda b, gg, qi: (b * nq + qi, gg)),
            pl.BlockSpec((tq, LANES), lambda b, gg, qi: (b * nq + qi, gg)),
            cmp_spec(0), cmp_spec(1),
            kv_spec(0), kv_spec(1), kv_spec(2), kv_spec(3),
            pl.BlockSpec((n_slc, hg * ncmp_pad), lambda b, gg, qi: (0, 0)),
            pl.BlockSpec((n_slc, t_len), lambda b, gg, qi: (0, 0)),
        ],
        out_specs=pl.BlockSpec((tq, hg * hd), lambda b, gg, qi: (b * nq + qi, gg)),
        out_shape=jax.ShapeDtypeStruct((bsz * t_len, g * hg * hd), BF16),
        scratch_shapes=[pltpu.VMEM((tq, t_len), F32)],
        compiler_params=_params("parallel", "parallel", "arbitrary"),
        name="nsa_attention",
    )(q, gates, kv_cmp, kv_cmp, kv_rest, kv_rest, kv_rest, kv_rest, ovt, expand)


def _rope_tables(pos, rot_dims, theta):
    half = rot_dims // 2
    inv_freq = jnp.power(jnp.float32(theta), -jnp.arange(half, dtype=F32) / half)
    ang = pos.astype(F32)[:, None] * inv_freq[None, :]
    cos, sin = jnp.cos(ang), jnp.sin(ang)
    n = pos.shape[0]
    pad = LANES - rot_dims
    c_tab = jnp.concatenate([cos, cos, jnp.ones((n, pad), F32)], axis=1)
    s_tab = jnp.concatenate([-sin, sin, jnp.zeros((n, pad), F32)], axis=1)
    return c_tab, s_tab


def kernel(x, ln_g, ln_b, ffn1_w_in, ffn1_w_out, ffn2_w_in, ffn2_w_out, ret_w_in, ret_gn_g, ret_gn_b, ret_w_out, kv_w, cmp_pos, cmp_w1, cmp_b1, cmp_w2, nsa_w_q, nsa_w_out):
    bsz, t_len, d = x.shape
    m = bsz * t_len
    bf = lambda w: w.astype(BF16)
    h = x.reshape(m, d)
    pos = jnp.arange(t_len)

    h, hb, w_in_next, w_out_next = _ffn(h, None, bf(ffn1_w_in[0]), bf(ffn1_w_out[0]), ln_g[0, 0], ln_b[0, 0],
                                        cast_next=(ffn2_w_in, ffn2_w_out, 0))
    dk = d // RET_HEADS
    dv = 2 * d // RET_HEADS
    half = dk // 2
    inv_freq = jnp.power(jnp.float32(RET_ROPE_BASE), -jnp.arange(half, dtype=F32) / half)
    ang = pos.astype(F32)[:, None] * inv_freq[None, :]
    n_ret = ret_w_in.shape[-1]
    ret_epi = functools.partial(_ret_epilogue, q_tiles=RET_HEADS * dk // 1024, k_scale=dk ** -0.5, half=half)
    proj = _wproj(hb, ret_w_in, col0=0, ncols=n_ret, epilogue=ret_epi,
                  out_spec=lambda tm, tn: pl.BlockSpec((tm, tn), lambda j, i: (i, j)),
                  out_shape=jax.ShapeDtypeStruct((m, n_ret), BF16),
                  extra=(jnp.cos(ang), jnp.sin(ang)), extra_specs=_table_specs(t_len, half, 1024), name="ret_proj")
    y = _retention(proj, ret_gn_g[0], ret_gn_b[0], bsz, t_len, dk, dv)
    h = _out_proj(y, bf(ret_w_out), h, ln_g[0, 1], ln_b[0, 1], "ret_out_proj")
    h, hb, w_in_next, w_out_next = _ffn(h, None, w_in_next, w_out_next, ln_g[0, 2], ln_b[0, 2],
                                        cast_next=(ffn1_w_in, ffn1_w_out, 1))

    hd = d // NSA_HEADS
    rope_dims = hd // 4
    rope_half = rope_dims // 2
    c_tab, s_tab = _rope_tables(pos, rope_dims, ROPE_THETA)
    w_kv = kv_w[None]
    slot_cols = NSA_KV_GROUPS * hd
    raw_cmp = _head_proj(hb, w_kv, c_tab, s_tab, t_len, col0=0, nslots=2, out_dtype=F32, half=rope_half,
                         scale=1.0, rotate="none", split=True, name="kv_cmp_proj")
    kv_rest = _head_proj(hb, w_kv, c_tab, s_tab, t_len, col0=2 * slot_cols, nslots=4, out_dtype=BF16,
                         half=rope_half, scale=1.0, rotate="even", split=True, name="kv_slc_win_proj")
    n_rows = t_len // CMP_STRIDE
    cmp_end = jnp.arange(n_rows) * CMP_STRIDE + CMP_BLOCK - 1
    cc_tab, cs_tab = _rope_tables(cmp_end, rope_dims, ROPE_THETA)
    kv_cmp = _cmp_mlp(raw_cmp, cmp_pos, bf(cmp_w1), cmp_b1, bf(cmp_w2), cc_tab, cs_tab, bsz, t_len, rope_half)

    h, hb, w_in_next, w_out_next = _ffn(h, None, w_in_next, w_out_next, ln_g[1, 0], ln_b[1, 0],
                                        cast_next=(ffn2_w_in, ffn2_w_out, 1))
    n_q = NSA_HEADS * hd
    q = _head_proj(hb, nsa_w_q, c_tab, s_tab, t_len, col0=0, nslots=n_q // slot_cols, out_dtype=BF16,
                   half=rope_half, scale=hd ** -0.5 * LOG2_E, rotate="all", split=False, name="nsa_q_proj")
    n_gate = NSA_HEADS_PER_GROUP * 3
    w_gate = nsa_w_q[0][:, n_q:].reshape(d, NSA_KV_GROUPS, n_gate)
    w_gate = jnp.pad(w_gate, ((0, 0), (0, 0), (0, LANES - n_gate))).reshape(1, d, NSA_KV_GROUPS * LANES)
    gates = _wproj(hb, w_gate, col0=0, ncols=NSA_KV_GROUPS * LANES, epilogue=_sigmoid_epilogue,
                   out_spec=lambda tm, tn: pl.BlockSpec((tm, tn), lambda j, i: (i, j)),
                   out_shape=jax.ShapeDtypeStruct((m, NSA_KV_GROUPS * LANES), F32), name="nsa_gate_proj")
    o = _nsa_attention(q, gates, kv_cmp, kv_rest, bsz, t_len)
    h = _out_proj(o, bf(nsa_w_out), h, ln_g[1, 1], ln_b[1, 1], "nsa_out_proj")
    (h,) = _ffn(h, None, w_in_next, w_out_next, ln_g[1, 2], ln_b[1, 2], emit_bf16=False)
    return h.reshape(bsz, t_len, d)
```

```python
import functools
import math

import numpy as np
import jax
import jax.numpy as jnp
from jax import lax
from jax.experimental import pallas as pl
from jax.experimental.pallas import tpu as pltpu

F32 = jnp.float32
BF16 = jnp.bfloat16

DEPTH = 2
N_A_LAYERS = DEPTH // 2
DEEPNORM_ALPHA = (2 * DEPTH) ** 0.25
LN_EPS = 1e-5
MACARON_WEIGHT = 0.5

RET_HEADS = 8
RET_CHUNK = 256
RET_ROPE_BASE = 10000.0

NSA_HEADS = 16
NSA_KV_GROUPS = 4
NSA_HEADS_PER_GROUP = NSA_HEADS // NSA_KV_GROUPS
CMP_BLOCK = 32
CMP_STRIDE = 16
SLC_BLOCK = 64
SLC_TOP = 8
WINDOW = 512
FORCE_BONUS = 1e4
ROPE_THETA = 500000.0
NEG_INF = -1e30
LOG2_E = math.log2(math.e)

V7X_VMEM_BYTES = 64 * 1024 * 1024
VMEM_LIMIT = V7X_VMEM_BYTES - 8 * 1024 * 1024
LANES = 128

FFN_SUBTILES = 2
NSA_Q_TILE = 256
NSA_KEY_CHUNK = 512


def _params(*semantics):
    return pltpu.CompilerParams(dimension_semantics=semantics, vmem_limit_bytes=VMEM_LIMIT)


def _layer_norm(y, g, b):
    mu = jnp.mean(y, axis=-1, keepdims=True)
    d = y - mu
    var = jnp.mean(d * d, axis=-1, keepdims=True)
    return d * lax.rsqrt(var + LN_EPS) * g + b


def _dot(a, b):
    return jnp.dot(a, b, preferred_element_type=F32)


def _dot_nt(a, b):
    return lax.dot_general(a, b, (((1,), (1,)), ((), ())), preferred_element_type=F32)


def _dot_tn(a, b):
    return lax.dot_general(a, b, (((0,), (0,)), ((), ())), preferred_element_type=F32)


def _ffn_kernel(*refs, nf, cast_next, emit_bf16):
    refs = list(refs)
    x_ref, wa_ref, wu_ref, wo_ref, g_ref, b_ref = refs[:6]
    refs = refs[6:]
    if cast_next:
        nwi_ref, nwo_ref = refs[:2]
        refs = refs[2:]
    o_ref = refs.pop(0)
    ob_ref = refs.pop(0) if emit_bf16 else None
    xb_ref = refs.pop()
    if cast_next:
        nwi_out, nwo_out = refs
        nwi_out[...] = nwi_ref[...].astype(BF16)
        nwo_out[...] = nwo_ref[...].astype(BF16)
    f = pl.program_id(1)

    @pl.when(f == 0)
    def _():
        xb_ref[...] = x_ref[...].astype(BF16)
        o_ref[...] = jnp.zeros_like(o_ref)

    xb = xb_ref[...]
    sub = wa_ref.shape[1] // FFN_SUBTILES
    part = None
    for s in range(FFN_SUBTILES):
        a = _dot(xb, wa_ref[:, s * sub:(s + 1) * sub])
        u = _dot(xb, wu_ref[:, s * sub:(s + 1) * sub])
        h = (a * jax.nn.sigmoid(a)) * u
        p = _dot(h.astype(BF16), wo_ref[s * sub:(s + 1) * sub, :])
        part = p if part is None else part + p
    o_ref[...] += part

    @pl.when(f == nf - 1)
    def _():
        y = DEEPNORM_ALPHA * x_ref[...] + MACARON_WEIGHT * o_ref[...]
        out = _layer_norm(y, g_ref[...], b_ref[...])
        o_ref[...] = out
        if emit_bf16:
            ob_ref[...] = out.astype(BF16)


def _ffn(x, w_in, w_out, g, b, cast_next=None, emit_bf16=False):
    m, d = x.shape
    f_dim = w_out.shape[0]
    tm, tf = 512, 512
    ni, nf = m // tm, f_dim // tf
    row_spec = pl.BlockSpec((tm, d), lambda i, f: (i, 0))
    vec_spec = pl.BlockSpec((1, d), lambda i, f: (0, 0))
    in_specs = [row_spec, pl.BlockSpec((d, tf), lambda i, f: (0, f)), pl.BlockSpec((d, tf), lambda i, f: (0, f + nf)),
                pl.BlockSpec((tf, d), lambda i, f: (f, 0)), vec_spec, vec_spec]
    args = [x, w_in, w_in, w_out, g.reshape(1, d), b.reshape(1, d)]
    out_specs, out_shape = [row_spec], [jax.ShapeDtypeStruct((m, d), F32)]
    if emit_bf16:
        out_specs.append(row_spec)
        out_shape.append(jax.ShapeDtypeStruct((m, d), BF16))
    if cast_next is not None:
        nw_in, nw_out, layer = cast_next
        ri, ci = d // ni, 2 * f_dim // nf
        ro = f_dim // (ni * nf)
        assert ri % 16 == 0 and ci % LANES == 0 and ro % 16 == 0
        in_specs += [pl.BlockSpec((None, ri, ci), lambda i, f: (layer, i, f)),
                     pl.BlockSpec((None, ro, d), lambda i, f: (layer, i * nf + f, 0))]
        out_specs += [pl.BlockSpec((ri, ci), lambda i, f: (i, f)), pl.BlockSpec((ro, d), lambda i, f: (i * nf + f, 0))]
        out_shape += [jax.ShapeDtypeStruct((d, 2 * f_dim), BF16), jax.ShapeDtypeStruct((f_dim, d), BF16)]
        args += [nw_in, nw_out]
    return pl.pallas_call(
        functools.partial(_ffn_kernel, nf=nf, cast_next=cast_next is not None, emit_bf16=emit_bf16),
        grid=(ni, nf),
        in_specs=in_specs,
        out_specs=out_specs,
        out_shape=out_shape,
        scratch_shapes=[pltpu.VMEM((tm, d), BF16)],
        compiler_params=_params("arbitrary", "arbitrary"),
        name="ffn_deepnorm",
    )(*args)


def _wproj_kernel(*refs, n_extra, epilogue):
    xb_ref, w_ref = refs[:2]
    extra = refs[2:2 + n_extra]
    o_ref, wb_ref = refs[2 + n_extra:]

    @pl.when(pl.program_id(1) == 0)
    def _():
        wb_ref[...] = w_ref[...].astype(BF16)

    epilogue(_dot(xb_ref[...], wb_ref[...]), pl.program_id(0), extra, o_ref)


def _wproj(xb, w, *, col0, ncols, epilogue, out_spec, out_shape, extra=(), extra_specs=(), name):
    m, k = xb.shape
    tm, tn = 1024, min(1024, ncols)
    assert col0 % tn == 0 and ncols % tn == 0
    return pl.pallas_call(
        functools.partial(_wproj_kernel, n_extra=len(extra), epilogue=epilogue),
        grid=(ncols // tn, m // tm),
        in_specs=[pl.BlockSpec((tm, k), lambda j, i: (i, 0)),
                  pl.BlockSpec((None, k, tn), lambda j, i: (0, 0, j + col0 // tn)), *extra_specs],
        out_specs=out_spec(tm, tn),
        out_shape=out_shape,
        scratch_shapes=[pltpu.VMEM((k, tn), BF16)],
        compiler_params=_params("arbitrary", "arbitrary"),
        name=name,
    )(xb, w, *extra)


def _table_specs(t_len, width, tm=1024):
    tpb = t_len // tm
    return [pl.BlockSpec((tm, width), lambda j, i: (i % tpb, 0))] * 2


def _ret_epilogue(acc, j, extra, o_ref, *, q_tiles, k_scale, half):
    cos_ref, sin_ref = extra
    tn = acc.shape[1]

    @pl.when(j < 2 * q_tiles)
    def _():
        scale = jnp.where(j >= q_tiles, k_scale, 1.0).astype(F32)
        cos = cos_ref[...]
        sin = sin_ref[...]
        for h in range(tn // (2 * half)):
            c0 = h * 2 * half
            x1 = acc[:, c0:c0 + half]
            x2 = acc[:, c0 + half:c0 + 2 * half]
            o_ref[:, c0:c0 + half] = ((x1 * cos - x2 * sin) * scale).astype(o_ref.dtype)
            o_ref[:, c0 + half:c0 + 2 * half] = ((x2 * cos + x1 * sin) * scale).astype(o_ref.dtype)

    @pl.when(j >= 2 * q_tiles)
    def _():
        o_ref[...] = acc.astype(o_ref.dtype)


def _partial_rotary(x, c, s, half):
    lane = lax.broadcasted_iota(jnp.int32, x.shape, 1)
    partner = jnp.where(lane < half, pltpu.roll(x, LANES - half, 1), pltpu.roll(x, half, 1))
    return x * c + partner * s


def _heads_epilogue(acc, j, extra, o_ref, *, hd, hps, half, scale, rotate, split):
    c_ref, s_ref = extra
    for h in range(acc.shape[1] // hd):
        piece = acc[:, h * hd:(h + 1) * hd]
        if rotate == "all" or (rotate == "even" and h < hps):
            piece = _partial_rotary(piece, c_ref[...], s_ref[...], half)
        if scale != 1.0:
            piece = piece * scale
        if split:
            o_ref[h // hps, h % hps] = piece.astype(o_ref.dtype)
        else:
            o_ref[:, h * hd:(h + 1) * hd] = piece.astype(o_ref.dtype)


def _sigmoid_epilogue(acc, j, extra, o_ref):
    o_ref[...] = jax.nn.sigmoid(acc).astype(o_ref.dtype)


def _head_proj(xb, w, c_tab, s_tab, t_len, *, col0, nslots, out_dtype, half, scale, rotate, split, name):
    m = xb.shape[0]
    hd, hps = LANES, NSA_KV_GROUPS
    assert nslots % 2 == 0
    if split:
        out_spec = lambda tm, tn: pl.BlockSpec((2, hps, tm, hd), lambda j, i: (j, 0, i, 0))
        out_shape = jax.ShapeDtypeStruct((nslots, hps, m, hd), out_dtype)
    else:
        out_spec = lambda tm, tn: pl.BlockSpec((tm, tn), lambda j, i: (i, j))
        out_shape = jax.ShapeDtypeStruct((m, nslots * hps * hd), out_dtype)
    epi = functools.partial(_heads_epilogue, hd=hd, hps=hps, half=half, scale=scale, rotate=rotate, split=split)
    return _wproj(xb, w, col0=col0, ncols=nslots * hps * hd, epilogue=epi, out_spec=out_spec, out_shape=out_shape,
                  extra=(c_tab, s_tab), extra_specs=_table_specs(t_len, hd), name=name)


def _out_proj_kernel(y_ref, w_ref, x_ref, g_ref, b_ref, o_ref, *, nsub):
    rs = y_ref.shape[0] // nsub
    for s in range(nsub):
        r = slice(s * rs, (s + 1) * rs)
        acc = _dot(y_ref[r, :], w_ref[...])
        o_ref[r, :] = _layer_norm(DEEPNORM_ALPHA * x_ref[r, :] + acc, g_ref[...], b_ref[...])


def _out_proj(y, w, x, g, b, name):
    m, k = y.shape
    d = w.shape[2]
    tm = 512
    row_spec = pl.BlockSpec((tm, d), lambda i: (i, 0))
    return pl.pallas_call(
        functools.partial(_out_proj_kernel, nsub=2),
        grid=(m // tm,),
        in_specs=[
            pl.BlockSpec((tm, k), lambda i: (i, 0)),
            pl.BlockSpec((None, k, d), lambda i: (0, 0, 0), pipeline_mode=pl.Buffered(1)),
            row_spec,
            pl.BlockSpec((1, d), lambda i: (0, 0)),
            pl.BlockSpec((1, d), lambda i: (0, 0)),
        ],
        out_specs=row_spec,
        out_shape=jax.ShapeDtypeStruct((m, d), F32),
        compiler_params=_params("parallel"),
        name=name,
    )(y, w, x, g.reshape(1, d), b.reshape(1, d))


def _retention_kernel(gam_ref, q_ref, k_ref, v_ref, g_ref, dec_ref, xi_ref, zeta_ref, gng_ref, gnb_ref,
                      o_ref, state_ref, *, nh, dk, dv):
    @pl.when(pl.program_id(1) == 0)
    def _():
        state_ref[...] = jnp.zeros_like(state_ref)

    for h in range(nh):
        q = q_ref[:, h * dk:(h + 1) * dk]
        k = k_ref[:, h * dk:(h + 1) * dk]
        v = v_ref[:, h * dv:(h + 1) * dv]
        state = state_ref[h]
        s = _dot_nt(q, k) * dec_ref[h]
        out = _dot(s.astype(BF16), v) + _dot(q, state.astype(BF16)) * xi_ref[h]
        kz = (k.astype(F32) * zeta_ref[h]).astype(BF16)
        state_ref[h] = _dot_tn(kz, v) + gam_ref[h] * state

        mu = jnp.mean(out, axis=-1, keepdims=True)
        d = out - mu
        var = jnp.mean(d * d, axis=-1, keepdims=True)
        cols = slice(h * dv, (h + 1) * dv)
        gn = d * lax.rsqrt(var + LN_EPS) * gng_ref[:, cols] + gnb_ref[:, cols]
        gate = g_ref[:, cols].astype(F32)
        o_ref[:, cols] = ((gate * jax.nn.sigmoid(gate)) * gn).astype(o_ref.dtype)


def _retention(proj, gn_g, gn_b, bsz, t_len, dk, dv):
    h = RET_HEADS
    c = RET_CHUNK
    nc = t_len // c
    m = bsz * t_len
    log_gamma = jnp.log(1.0 - jnp.power(2.0, -5.0 - jnp.arange(h, dtype=F32)))
    idx = jnp.arange(c, dtype=F32)
    diff = idx[:, None] - idx[None, :]
    decay = jnp.where(diff >= 0, jnp.exp(log_gamma[:, None, None] * jnp.maximum(diff, 0.0)), 0.0)
    xi = jnp.exp(log_gamma[:, None] * (idx + 1.0))[..., None]
    zeta = jnp.exp(log_gamma[:, None] * (c - 1.0 - idx))[..., None]
    gamma_c = jnp.exp(log_gamma * c)
    row = lambda b, cc: b * nc + cc
    whole = lambda shape: pl.BlockSpec(shape, lambda b, cc: (0,) * len(shape))
    v_blk = 2 * dk // dv
    return pl.pallas_call(
        functools.partial(_retention_kernel, nh=h, dk=dk, dv=dv),
        grid=(bsz, nc),
        in_specs=[
            pl.BlockSpec(memory_space=pltpu.SMEM),
            pl.BlockSpec((c, h * dk), lambda b, cc: (row(b, cc), 0)),
            pl.BlockSpec((c, h * dk), lambda b, cc: (row(b, cc), 1)),
            pl.BlockSpec((c, h * dv), lambda b, cc: (row(b, cc), v_blk)),
            pl.BlockSpec((c, h * dv), lambda b, cc: (row(b, cc), v_blk + 1)),
            whole((h, c, c)), whole((h, c, 1)), whole((h, c, 1)),
            whole((1, h * dv)), whole((1, h * dv)),
        ],
        out_specs=pl.BlockSpec((c, h * dv), lambda b, cc: (row(b, cc), 0)),
        out_shape=jax.ShapeDtypeStruct((m, h * dv), BF16),
        scratch_shapes=[pltpu.VMEM((h, dk, dv), F32)],
        compiler_params=_params("parallel", "arbitrary"),
        name="retention_chunks",
    )(gamma_c, proj, proj, proj, proj, decay, xi, zeta, gn_g.reshape(1, h * dv), gn_b.reshape(1, h * dv))


def _cmp_mlp_kernel(x_ref, pos_ref, w1a_ref, w1b_ref, b1_ref, w2_ref, c_ref, s_ref, o_ref, *, half):
    nrows = o_ref.shape[0]
    x = jnp.concatenate([x_ref[pl.ds(l, nrows, stride=CMP_STRIDE), :] for l in range(CMP_STRIDE)], axis=1)
    pos = pos_ref[...]
    a = _dot((x + pos[0:1, :]).astype(BF16), w1a_ref[...])
    bm = _dot((x + pos[1:2, :]).astype(BF16), w1b_ref[...])
    hid = jax.nn.gelu(a + pltpu.roll(bm, nrows - 1, 0) + b1_ref[...])
    comp = _dot(hid.astype(BF16), w2_ref[...])
    is_key = pl.program_id(0) == 0
    c = jnp.where(is_key, c_ref[...], 1.0)
    s = jnp.where(is_key, s_ref[...], 0.0)
    o_ref[...] = _partial_rotary(comp, c, s, half).astype(o_ref.dtype)


def _cmp_mlp(raw, cmp_pos, w1, b1, w2, c_tab, s_tab, bsz, t_len, half):
    g, hd = NSA_KV_GROUPS, LANES
    nrows = t_len // CMP_STRIDE
    feat = CMP_STRIDE * hd
    hidden = w1.shape[-1]
    pos = cmp_pos.reshape(2, CMP_BLOCK // CMP_STRIDE, feat)
    return pl.pallas_call(
        functools.partial(_cmp_mlp_kernel, half=half),
        grid=(2, g, bsz),
        in_specs=[
            pl.BlockSpec((None, None, t_len, hd), lambda c, gg, b: (c, gg, b, 0)),
            pl.BlockSpec((None, 2, feat), lambda c, gg, b: (c, 0, 0)),
            pl.BlockSpec((None, feat, hidden), lambda c, gg, b: (c, 0, 0)),
            pl.BlockSpec((None, feat, hidden), lambda c, gg, b: (c, 1, 0)),
            pl.BlockSpec((None, 1, hidden), lambda c, gg, b: (c, 0, 0)),
            pl.BlockSpec((None, hidden, hd), lambda c, gg, b: (c, 0, 0)),
            pl.BlockSpec((nrows, hd), lambda c, gg, b: (0, 0)),
            pl.BlockSpec((nrows, hd), lambda c, gg, b: (0, 0)),
        ],
        out_specs=pl.BlockSpec((None, None, None, nrows, hd), lambda c, gg, b: (c, gg, b, 0, 0)),
        out_shape=jax.ShapeDtypeStruct((2, g, bsz, nrows, hd), BF16),
        compiler_params=_params("arbitrary", "arbitrary", "arbitrary"),
        name="cmp_mlp",
    )(raw, pos, w1, w1, b1.reshape(2, 1, hidden), w2, c_tab, s_tab)


def _with_ones(v):
    return jnp.concatenate([v, jnp.ones_like(v)], axis=1)


def _nsa_kernel(q_ref, gate_ref, kc_ref, vc_ref, ks_ref, vs_ref, kw_ref, vw_ref, ovt_ref, ex_ref,
                o_ref, mask_ref, *, tq, tc, hg, hd, max_chunks):
    qi = pl.program_id(2)
    t0 = qi * tq
    rows = hg * tq
    q = q_ref[...]
    q4 = jnp.concatenate([q[:, i * hd:(i + 1) * hd] for i in range(hg)], axis=0)
    tpos = t0 + lax.broadcasted_iota(jnp.int32, (tq, 1), 0)

    ncmp = kc_ref.shape[0]
    s3 = _dot_nt(q4, kc_ref[...]).reshape(hg, tq, ncmp)
    cmp_end = lax.broadcasted_iota(jnp.int32, (tq, ncmp), 1) * CMP_STRIDE + (CMP_BLOCK - 1)
    s3 = jnp.where((cmp_end <= tpos)[None], s3, NEG_INF)
    e3 = jnp.exp2(s3 - jnp.max(s3, axis=-1, keepdims=True))
    has_cmp = (tpos >= CMP_BLOCK - 1).astype(F32)
    p3 = (e3 / jnp.sum(e3, axis=-1, keepdims=True) * has_cmp[None]).astype(BF16)
    o_cmp = _dot(p3.reshape(rows, ncmp), vc_ref[...])

    nslc = ovt_ref.shape[0]
    p_slc = _dot_nt(ovt_ref[...], jnp.concatenate([p3[i] for i in range(hg)], axis=1))
    trow = t0 + lax.broadcasted_iota(jnp.int32, (1, tq), 1)
    blk = lax.broadcasted_iota(jnp.int32, (nslc, tq), 0)
    cur = jnp.right_shift(trow, SLC_BLOCK.bit_length() - 1)
    valid = blk * SLC_BLOCK <= trow
    forced = (blk == 0) | (blk == cur) | (blk == cur - 1)
    score = jnp.where(valid, p_slc + jnp.where(forced, FORCE_BONUS, 0.0), -1.0)
    rank = jnp.zeros((nslc, tq), jnp.int32)
    for kb in range(nslc):
        ck = score[kb:kb + 1, :]
        rank += ((ck > score) | ((ck == score) & (blk > kb))).astype(jnp.int32)
    sel = jnp.where(rank < SLC_TOP, 1.0, 0.0).astype(BF16)
    mask_ref[...] = _dot_tn(sel, ex_ref[...])

    gts = gate_ref[...]

    def window_and_selected(n_chunks):
        wlen = WINDOW + tq
        w0 = pl.multiple_of(jnp.maximum(qi - WINDOW // tq, 0) * tq, tq)
        sw = _dot_nt(q4, kw_ref[pl.ds(w0, wlen), :]).reshape(hg, tq, wlen)
        kpos = w0 + lax.broadcasted_iota(jnp.int32, (tq, wlen), 1)
        okw = (kpos <= tpos) & (kpos > tpos - WINDOW)
        sw = jnp.where(okw[None], sw, NEG_INF)
        ew = jnp.exp2(sw - jnp.max(sw, axis=-1, keepdims=True)).astype(BF16)
        rw = _dot(ew.reshape(rows, wlen), _with_ones(vw_ref[pl.ds(w0, wlen), :]))
        o_win = rw[:, :hd] / rw[:, hd:]

        m_i = acc = None
        for c in range(n_chunks):
            k0 = c * tc
            s = _dot_nt(q4, ks_ref[k0:k0 + tc, :]).reshape(hg, tq, tc)
            ok = mask_ref[:, k0:k0 + tc] > 0.5
            if c == n_chunks - 1:
                ok &= (k0 + lax.broadcasted_iota(jnp.int32, (tq, tc), 1)) <= tpos
            s = jnp.where(ok[None], s, NEG_INF)
            m_c = jnp.max(s, axis=-1, keepdims=True)
            m_new = m_c if c == 0 else jnp.maximum(m_i, m_c)
            pv = _dot(jnp.exp2(s - m_new).astype(BF16).reshape(rows, tc), _with_ones(vs_ref[k0:k0 + tc, :]))
            acc = pv if c == 0 else acc * jnp.exp2(m_i - m_new).reshape(rows, 1) + pv
            m_i = m_new
        o_slc = acc[:, :hd] / acc[:, hd:]

        for i in range(hg):
            r = slice(i * tq, (i + 1) * tq)
            o = (gts[:, 3 * i:3 * i + 1] * o_cmp[r] + gts[:, 3 * i + 1:3 * i + 2] * o_slc[r]
                 + gts[:, 3 * i + 2:3 * i + 3] * o_win[r])
            o_ref[:, i * hd:(i + 1) * hd] = o.astype(o_ref.dtype)

    n_chunks = lax.div(t0, tc) + 1
    for n in range(1, max_chunks + 1):
        pl.when(n_chunks == n)(functools.partial(window_and_selected, n))


def _nsa_attention(q, gates, kv_cmp, kv_rest, bsz, t_len):
    g, hg, hd = NSA_KV_GROUPS, NSA_HEADS_PER_GROUP, LANES
    tq, tc = NSA_Q_TILE, NSA_KEY_CHUNK
    nq = t_len // tq
    ncmp_pad = t_len // CMP_STRIDE
    n_cmp = (t_len - CMP_BLOCK) // CMP_STRIDE + 1
    n_slc = t_len // SLC_BLOCK
    cs = np.arange(ncmp_pad)[None, :] * CMP_STRIDE
    ss = np.arange(n_slc)[:, None] * SLC_BLOCK
    ovt = ((cs < ss + SLC_BLOCK) & (cs + CMP_BLOCK > ss) & (np.arange(ncmp_pad)[None, :] < n_cmp)).astype(np.float32)
    ovt = jnp.asarray(np.tile(ovt, (1, hg)), BF16)
    expand = jnp.asarray((np.arange(t_len)[None, :] // SLC_BLOCK == np.arange(n_slc)[:, None]).astype(np.float32), BF16)
    kv_spec = lambda slot: pl.BlockSpec((None, None, t_len, hd), lambda b, gg, qi: (slot, gg, b, 0))
    cmp_spec = lambda slot: pl.BlockSpec((None, None, None, ncmp_pad, hd), lambda b, gg, qi: (slot, gg, b, 0, 0))
    return pl.pallas_call(
        functools.partial(_nsa_kernel, tq=tq, tc=tc, hg=hg, hd=hd, max_chunks=t_len // tc),
        grid=(bsz, g, nq),
        in_specs=[
            pl.BlockSpec((tq, hg * hd), lambda b, gg, qi: (b * nq + qi, gg)),
            pl.BlockSpec((tq, LANES), lambda b, gg, qi: (b * nq + qi, gg)),
            cmp_spec(0), cmp_spec(1),
            kv_spec(0), kv_spec(1), kv_spec(2), kv_spec(3),
            pl.BlockSpec((n_slc, hg * ncmp_pad), lambda b, gg, qi: (0, 0)),
            pl.BlockSpec((n_slc, t_len), lambda b, gg, qi: (0, 0)),
        ],
        out_specs=pl.BlockSpec((tq, hg * hd), lambda b, gg, qi: (b * nq + qi, gg)),
        out_shape=jax.ShapeDtypeStruct((bsz * t_len, g * hg * hd), BF16),
        scratch_shapes=[pltpu.VMEM((tq, t_len), F32)],
        compiler_params=_params("parallel", "parallel", "arbitrary"),
        name="nsa_attention",
    )(q, gates, kv_cmp, kv_cmp, kv_rest, kv_rest, kv_rest, kv_rest, ovt, expand)


def _rope_tables(pos, rot_dims, theta):
    half = rot_dims // 2
    inv_freq = jnp.power(jnp.float32(theta), -jnp.arange(half, dtype=F32) / half)
    ang = pos.astype(F32)[:, None] * inv_freq[None, :]
    cos, sin = jnp.cos(ang), jnp.sin(ang)
    n = pos.shape[0]
    pad = LANES - rot_dims
    c_tab = jnp.concatenate([cos, cos, jnp.ones((n, pad), F32)], axis=1)
    s_tab = jnp.concatenate([-sin, sin, jnp.zeros((n, pad), F32)], axis=1)
    return c_tab, s_tab


def kernel(x, ln_g, ln_b, ffn1_w_in, ffn1_w_out, ffn2_w_in, ffn2_w_out, ret_w_in, ret_gn_g, ret_gn_b, ret_w_out, kv_w, cmp_pos, cmp_w1, cmp_b1, cmp_w2, nsa_w_q, nsa_w_out):
    bsz, t_len, d = x.shape
    m = bsz * t_len
    bf = lambda w: w.astype(BF16)
    h = x.reshape(m, d)
    pos = jnp.arange(t_len)

    h, hb, w_in_next, w_out_next = _ffn(h, bf(ffn1_w_in[0]), bf(ffn1_w_out[0]), ln_g[0, 0], ln_b[0, 0],
                                        cast_next=(ffn2_w_in, ffn2_w_out, 0), emit_bf16=True)
    dk = d // RET_HEADS
    dv = 2 * d // RET_HEADS
    half = dk // 2
    inv_freq = jnp.power(jnp.float32(RET_ROPE_BASE), -jnp.arange(half, dtype=F32) / half)
    ang = pos.astype(F32)[:, None] * inv_freq[None, :]
    n_ret = ret_w_in.shape[-1]
    ret_epi = functools.partial(_ret_epilogue, q_tiles=RET_HEADS * dk // 1024, k_scale=dk ** -0.5, half=half)
    proj = _wproj(hb, ret_w_in, col0=0, ncols=n_ret, epilogue=ret_epi,
                  out_spec=lambda tm, tn: pl.BlockSpec((tm, tn), lambda j, i: (i, j)),
                  out_shape=jax.ShapeDtypeStruct((m, n_ret), BF16),
                  extra=(jnp.cos(ang), jnp.sin(ang)), extra_specs=_table_specs(t_len, half), name="ret_proj")
    y = _retention(proj, ret_gn_g[0], ret_gn_b[0], bsz, t_len, dk, dv)
    h = _out_proj(y, bf(ret_w_out), h, ln_g[0, 1], ln_b[0, 1], "ret_out_proj")
    h, hb, w_in_next, w_out_next = _ffn(h, w_in_next, w_out_next, ln_g[0, 2], ln_b[0, 2],
                                        cast_next=(ffn1_w_in, ffn1_w_out, 1), emit_bf16=True)

    hd = d // NSA_HEADS
    rope_dims = hd // 4
    rope_half = rope_dims // 2
    c_tab, s_tab = _rope_tables(pos, rope_dims, ROPE_THETA)
    w_kv = kv_w[None]
    slot_cols = NSA_KV_GROUPS * hd
    raw_cmp = _head_proj(hb, w_kv, c_tab, s_tab, t_len, col0=0, nslots=2, out_dtype=F32, half=rope_half,
                         scale=1.0, rotate="none", split=True, name="kv_cmp_proj")
    kv_rest = _head_proj(hb, w_kv, c_tab, s_tab, t_len, col0=2 * slot_cols, nslots=4, out_dtype=BF16,
                         half=rope_half, scale=1.0, rotate="even", split=True, name="kv_slc_win_proj")
    n_rows = t_len // CMP_STRIDE
    cmp_end = jnp.arange(n_rows) * CMP_STRIDE + CMP_BLOCK - 1
    cc_tab, cs_tab = _rope_tables(cmp_end, rope_dims, ROPE_THETA)
    kv_cmp = _cmp_mlp(raw_cmp, cmp_pos, bf(cmp_w1), cmp_b1, bf(cmp_w2), cc_tab, cs_tab, bsz, t_len, rope_half)

    h, hb, w_in_next, w_out_next = _ffn(h, w_in_next, w_out_next, ln_g[1, 0], ln_b[1, 0],
                                        cast_next=(ffn2_w_in, ffn2_w_out, 1), emit_bf16=True)
    n_q = NSA_HEADS * hd
    q = _head_proj(hb, nsa_w_q, c_tab, s_tab, t_len, col0=0, nslots=n_q // slot_cols, out_dtype=BF16,
                   half=rope_half, scale=hd ** -0.5 * LOG2_E, rotate="all", split=False, name="nsa_q_proj")
    n_gate = NSA_HEADS_PER_GROUP * 3
    w_gate = nsa_w_q[0][:, n_q:].reshape(d, NSA_KV_GROUPS, n_gate)
    w_gate = jnp.pad(w_gate, ((0, 0), (0, 0), (0, LANES - n_gate))).reshape(1, d, NSA_KV_GROUPS * LANES)
    gates = _wproj(hb, w_gate, col0=0, ncols=NSA_KV_GROUPS * LANES, epilogue=_sigmoid_epilogue,
                   out_spec=lambda tm, tn: pl.BlockSpec((tm, tn), lambda j, i: (i, j)),
                   out_shape=jax.ShapeDtypeStruct((m, NSA_KV_GROUPS * LANES), F32), name="nsa_gate_proj")
    o = _nsa_attention(q, gates, kv_cmp, kv_rest, bsz, t_len)
    h = _out_proj(o, bf(nsa_w_out), h, ln_g[1, 1], ln_b[1, 1], "nsa_out_proj")
    (h,) = _ffn(h, w_in_next, w_out_next, ln_g[1, 2], ln_b[1, 2])
    return h.reshape(bsz, t_len, d)
```

```python
import functools
import math

import numpy as np
import jax
import jax.numpy as jnp
from jax import lax
from jax.experimental import pallas as pl
from jax.experimental.pallas import tpu as pltpu

F32 = jnp.float32
BF16 = jnp.bfloat16

DEPTH = 2
N_A_LAYERS = DEPTH // 2
DEEPNORM_ALPHA = (2 * DEPTH) ** 0.25
LN_EPS = 1e-5
MACARON_WEIGHT = 0.5

RET_HEADS = 8
RET_CHUNK = 256
RET_ROPE_BASE = 10000.0

NSA_HEADS = 16
NSA_KV_GROUPS = 4
NSA_HEADS_PER_GROUP = NSA_HEADS // NSA_KV_GROUPS
CMP_BLOCK = 32
CMP_STRIDE = 16
SLC_BLOCK = 64
SLC_TOP = 8
WINDOW = 512
FORCE_BONUS = 1e4
ROPE_THETA = 500000.0
NEG_INF = -1e30
LOG2_E = math.log2(math.e)

V7X_VMEM_BYTES = 64 * 1024 * 1024
VMEM_LIMIT = V7X_VMEM_BYTES - 8 * 1024 * 1024
LANES = 128

FFN_SUBTILES = 2
NSA_Q_TILE = 256
NSA_KEY_CHUNK = 512


def _params(*semantics):
    return pltpu.CompilerParams(dimension_semantics=semantics, vmem_limit_bytes=VMEM_LIMIT)


def _layer_norm(y, g, b):
    mu = jnp.mean(y, axis=-1, keepdims=True)
    d = y - mu
    var = jnp.mean(d * d, axis=-1, keepdims=True)
    return d * lax.rsqrt(var + LN_EPS) * g + b


def _dot(a, b):
    return jnp.dot(a, b, preferred_element_type=F32)


def _dot_nt(a, b):
    return lax.dot_general(a, b, (((1,), (1,)), ((), ())), preferred_element_type=F32)


def _dot_tn(a, b):
    return lax.dot_general(a, b, (((0,), (0,)), ((), ())), preferred_element_type=F32)


def _ffn_kernel(*refs, nf, cast_next, emit_bf16):
    refs = list(refs)
    x_ref, wa_ref, wu_ref, wo_ref, g_ref, b_ref = refs[:6]
    refs = refs[6:]
    if cast_next:
        nwi_ref, nwo_ref = refs[:2]
        refs = refs[2:]
    o_ref = refs.pop(0)
    ob_ref = refs.pop(0) if emit_bf16 else None
    xb_ref = refs.pop()
    if cast_next:
        nwi_out, nwo_out = refs
        nwi_out[...] = nwi_ref[...].astype(BF16)
        nwo_out[...] = nwo_ref[...].astype(BF16)
    f = pl.program_id(1)

    @pl.when(f == 0)
    def _():
        xb_ref[...] = x_ref[...].astype(BF16)
        o_ref[...] = jnp.zeros_like(o_ref)

    xb = xb_ref[...]
    sub = wa_ref.shape[1] // FFN_SUBTILES
    part = None
    for s in range(FFN_SUBTILES):
        a = _dot(xb, wa_ref[:, s * sub:(s + 1) * sub])
        u = _dot(xb, wu_ref[:, s * sub:(s + 1) * sub])
        h = (a * jax.nn.sigmoid(a)) * u
        p = _dot(h.astype(BF16), wo_ref[s * sub:(s + 1) * sub, :])
        part = p if part is None else part + p
    o_ref[...] += part

    @pl.when(f == nf - 1)
    def _():
        y = DEEPNORM_ALPHA * x_ref[...] + MACARON_WEIGHT * o_ref[...]
        out = _layer_norm(y, g_ref[...], b_ref[...])
        o_ref[...] = out
        if emit_bf16:
            ob_ref[...] = out.astype(BF16)


def _ffn(x, w_in, w_out, g, b, cast_next=None, emit_bf16=False):
    m, d = x.shape
    f_dim = w_out.shape[0]
    tm, tf = 512, 512
    ni, nf = m // tm, f_dim // tf
    row_spec = pl.BlockSpec((tm, d), lambda i, f: (i, 0))
    vec_spec = pl.BlockSpec((1, d), lambda i, f: (0, 0))
    in_specs = [row_spec, pl.BlockSpec((d, tf), lambda i, f: (0, f)), pl.BlockSpec((d, tf), lambda i, f: (0, f + nf)),
                pl.BlockSpec((tf, d), lambda i, f: (f, 0)), vec_spec, vec_spec]
    args = [x, w_in, w_in, w_out, g.reshape(1, d), b.reshape(1, d)]
    out_specs, out_shape = [row_spec], [jax.ShapeDtypeStruct((m, d), F32)]
    if emit_bf16:
        out_specs.append(row_spec)
        out_shape.append(jax.ShapeDtypeStruct((m, d), BF16))
    if cast_next is not None:
        nw_in, nw_out, layer = cast_next
        ri, ci = d // ni, 2 * f_dim // nf
        ro = f_dim // (ni * nf)
        assert ri % 16 == 0 and ci % LANES == 0 and ro % 16 == 0
        in_specs += [pl.BlockSpec((None, ri, ci), lambda i, f: (layer, i, f)),
                     pl.BlockSpec((None, ro, d), lambda i, f: (layer, i * nf + f, 0))]
        out_specs += [pl.BlockSpec((ri, ci), lambda i, f: (i, f)), pl.BlockSpec((ro, d), lambda i, f: (i * nf + f, 0))]
        out_shape += [jax.ShapeDtypeStruct((d, 2 * f_dim), BF16), jax.ShapeDtypeStruct((f_dim, d), BF16)]
        args += [nw_in, nw_out]
    return pl.pallas_call(
        functools.partial(_ffn_kernel, nf=nf, cast_next=cast_next is not None, emit_bf16=emit_bf16),
        grid=(ni, nf),
        in_specs=in_specs,
        out_specs=out_specs,
        out_shape=out_shape,
        scratch_shapes=[pltpu.VMEM((tm, d), BF16)],
        compiler_params=_params("arbitrary", "arbitrary"),
        name="ffn_deepnorm",
    )(*args)


def _wproj_kernel(*refs, n_extra, epilogue):
    xb_ref, w_ref = refs[:2]
    extra = refs[2:2 + n_extra]
    o_ref, wb_ref = refs[2 + n_extra:]

    @pl.when(pl.program_id(1) == 0)
    def _():
        wb_ref[...] = w_ref[...].astype(BF16)

    epilogue(_dot(xb_ref[...], wb_ref[...]), pl.program_id(0), extra, o_ref)


def _wproj(xb, w, *, col0, ncols, epilogue, out_spec, out_shape, extra=(), extra_specs=(), name):
    m, k = xb.shape
    tm, tn = 1024, min(1024, ncols)
    assert col0 % tn == 0 and ncols % tn == 0
    return pl.pallas_call(
        functools.partial(_wproj_kernel, n_extra=len(extra), epilogue=epilogue),
        grid=(ncols // tn, m // tm),
        in_specs=[pl.BlockSpec((tm, k), lambda j, i: (i, 0)),
                  pl.BlockSpec((None, k, tn), lambda j, i: (0, 0, j + col0 // tn)), *extra_specs],
        out_specs=out_spec(tm, tn),
        out_shape=out_shape,
        scratch_shapes=[pltpu.VMEM((k, tn), BF16)],
        compiler_params=_params("arbitrary", "arbitrary"),
        name=name,
    )(xb, w, *extra)


def _table_specs(t_len, width, tm=1024):
    tpb = t_len // tm
    return [pl.BlockSpec((tm, width), lambda j, i: (i % tpb, 0))] * 2


def _ret_epilogue(acc, j, extra, o_ref, *, q_tiles, gate_tile0, k_scale, half):
    cos_ref, sin_ref = extra
    tn = acc.shape[1]

    @pl.when(j < 2 * q_tiles)
    def _():
        scale = jnp.where(j >= q_tiles, k_scale, 1.0).astype(F32)
        cos = cos_ref[...]
        sin = sin_ref[...]
        for h in range(tn // (2 * half)):
            c0 = h * 2 * half
            x1 = acc[:, c0:c0 + half]
            x2 = acc[:, c0 + half:c0 + 2 * half]
            o_ref[:, c0:c0 + half] = ((x1 * cos - x2 * sin) * scale).astype(o_ref.dtype)
            o_ref[:, c0 + half:c0 + 2 * half] = ((x2 * cos + x1 * sin) * scale).astype(o_ref.dtype)

    @pl.when((j >= 2 * q_tiles) & (j < gate_tile0))
    def _():
        o_ref[...] = acc.astype(o_ref.dtype)

    @pl.when(j >= gate_tile0)
    def _():
        o_ref[...] = (acc * jax.nn.sigmoid(acc)).astype(o_ref.dtype)


def _partial_rotary(x, c, s, half):
    lane = lax.broadcasted_iota(jnp.int32, x.shape, 1)
    partner = jnp.where(lane < half, pltpu.roll(x, LANES - half, 1), pltpu.roll(x, half, 1))
    return x * c + partner * s


def _heads_epilogue(acc, j, extra, o_ref, *, hd, hps, half, scale, rotate, split):
    c_ref, s_ref = extra
    for h in range(acc.shape[1] // hd):
        piece = acc[:, h * hd:(h + 1) * hd]
        if rotate == "all" or (rotate == "even" and h < hps):
            piece = _partial_rotary(piece, c_ref[...], s_ref[...], half)
        if scale != 1.0:
            piece = piece * scale
        if split:
            o_ref[h // hps, h % hps] = piece.astype(o_ref.dtype)
        else:
            o_ref[:, h * hd:(h + 1) * hd] = piece.astype(o_ref.dtype)


def _sigmoid_epilogue(acc, j, extra, o_ref):
    o_ref[...] = jax.nn.sigmoid(acc).astype(o_ref.dtype)


def _head_proj(xb, w, c_tab, s_tab, t_len, *, col0, nslots, out_dtype, half, scale, rotate, split, name):
    m = xb.shape[0]
    hd, hps = LANES, NSA_KV_GROUPS
    assert nslots % 2 == 0
    if split:
        out_spec = lambda tm, tn: pl.BlockSpec((2, hps, tm, hd), lambda j, i: (j, 0, i, 0))
        out_shape = jax.ShapeDtypeStruct((nslots, hps, m, hd), out_dtype)
    else:
        out_spec = lambda tm, tn: pl.BlockSpec((tm, tn), lambda j, i: (i, j))
        out_shape = jax.ShapeDtypeStruct((m, nslots * hps * hd), out_dtype)
    epi = functools.partial(_heads_epilogue, hd=hd, hps=hps, half=half, scale=scale, rotate=rotate, split=split)
    return _wproj(xb, w, col0=col0, ncols=nslots * hps * hd, epilogue=epi, out_spec=out_spec, out_shape=out_shape,
                  extra=(c_tab, s_tab), extra_specs=_table_specs(t_len, hd), name=name)


def _out_proj_kernel(y_ref, w_ref, x_ref, g_ref, b_ref, o_ref, *, nsub):
    rs = y_ref.shape[0] // nsub
    for s in range(nsub):
        r = slice(s * rs, (s + 1) * rs)
        acc = _dot(y_ref[r, :], w_ref[...])
        o_ref[r, :] = _layer_norm(DEEPNORM_ALPHA * x_ref[r, :] + acc, g_ref[...], b_ref[...])


def _out_proj(y, w, x, g, b, name):
    m, k = y.shape
    d = w.shape[2]
    tm = 512
    row_spec = pl.BlockSpec((tm, d), lambda i: (i, 0))
    return pl.pallas_call(
        functools.partial(_out_proj_kernel, nsub=2),
        grid=(m // tm,),
        in_specs=[
            pl.BlockSpec((tm, k), lambda i: (i, 0)),
            pl.BlockSpec((None, k, d), lambda i: (0, 0, 0), pipeline_mode=pl.Buffered(1)),
            row_spec,
            pl.BlockSpec((1, d), lambda i: (0, 0)),
            pl.BlockSpec((1, d), lambda i: (0, 0)),
        ],
        out_specs=row_spec,
        out_shape=jax.ShapeDtypeStruct((m, d), F32),
        compiler_params=_params("parallel"),
        name=name,
    )(y, w, x, g.reshape(1, d), b.reshape(1, d))


def _retention_kernel(gam_ref, q_ref, k_ref, v_ref, g_ref, dec_ref, xi_ref, zeta_ref, gng_ref, gnb_ref,
                      o_ref, state_ref, *, nh, dk, dv):
    @pl.when(pl.program_id(1) == 0)
    def _():
        state_ref[...] = jnp.zeros_like(state_ref)

    for h in range(nh):
        q = q_ref[:, h * dk:(h + 1) * dk]
        k = k_ref[:, h * dk:(h + 1) * dk]
        v = v_ref[:, h * dv:(h + 1) * dv]
        state = state_ref[h]
        s = _dot_nt(q, k) * dec_ref[h]
        out = _dot(s.astype(BF16), v) + _dot(q, state.astype(BF16)) * xi_ref[h]
        kz = (k.astype(F32) * zeta_ref[h]).astype(BF16)
        state_ref[h] = _dot_tn(kz, v) + gam_ref[h] * state

        mu = jnp.mean(out, axis=-1, keepdims=True)
        d = out - mu
        var = jnp.mean(d * d, axis=-1, keepdims=True)
        cols = slice(h * dv, (h + 1) * dv)
        gn = d * lax.rsqrt(var + LN_EPS) * gng_ref[:, cols] + gnb_ref[:, cols]
        o_ref[:, cols] = (g_ref[:, cols].astype(F32) * gn).astype(o_ref.dtype)


def _retention(proj, gn_g, gn_b, bsz, t_len, dk, dv):
    h = RET_HEADS
    c = RET_CHUNK
    nc = t_len // c
    m = bsz * t_len
    log_gamma = jnp.log(1.0 - jnp.power(2.0, -5.0 - jnp.arange(h, dtype=F32)))
    idx = jnp.arange(c, dtype=F32)
    diff = idx[:, None] - idx[None, :]
    decay = jnp.where(diff >= 0, jnp.exp(log_gamma[:, None, None] * jnp.maximum(diff, 0.0)), 0.0)
    xi = jnp.exp(log_gamma[:, None] * (idx + 1.0))[..., None]
    zeta = jnp.exp(log_gamma[:, None] * (c - 1.0 - idx))[..., None]
    gamma_c = jnp.exp(log_gamma * c)
    row = lambda b, cc: b * nc + cc
    whole = lambda shape: pl.BlockSpec(shape, lambda b, cc: (0,) * len(shape))
    v_blk = 2 * dk // dv
    return pl.pallas_call(
        functools.partial(_retention_kernel, nh=h, dk=dk, dv=dv),
        grid=(bsz, nc),
        in_specs=[
            pl.BlockSpec(memory_space=pltpu.SMEM),
            pl.BlockSpec((c, h * dk), lambda b, cc: (row(b, cc), 0)),
            pl.BlockSpec((c, h * dk), lambda b, cc: (row(b, cc), 1)),
            pl.BlockSpec((c, h * dv), lambda b, cc: (row(b, cc), v_blk)),
            pl.BlockSpec((c, h * dv), lambda b, cc: (row(b, cc), v_blk + 1)),
            whole((h, c, c)), whole((h, c, 1)), whole((h, c, 1)),
            whole((1, h * dv)), whole((1, h * dv)),
        ],
        out_specs=pl.BlockSpec((c, h * dv), lambda b, cc: (row(b, cc), 0)),
        out_shape=jax.ShapeDtypeStruct((m, h * dv), BF16),
        scratch_shapes=[pltpu.VMEM((h, dk, dv), F32)],
        compiler_params=_params("parallel", "arbitrary"),
        name="retention_chunks",
    )(gamma_c, proj, proj, proj, proj, decay, xi, zeta, gn_g.reshape(1, h * dv), gn_b.reshape(1, h * dv))


def _cmp_mlp_kernel(x_ref, pos_ref, w1a_ref, w1b_ref, b1_ref, w2_ref, c_ref, s_ref, o_ref, *, half):
    nrows = o_ref.shape[0]
    x = jnp.concatenate([x_ref[pl.ds(l, nrows, stride=CMP_STRIDE), :] for l in range(CMP_STRIDE)], axis=1)
    pos = pos_ref[...]
    a = _dot((x + pos[0:1, :]).astype(BF16), w1a_ref[...])
    bm = _dot((x + pos[1:2, :]).astype(BF16), w1b_ref[...])
    hid = jax.nn.gelu(a + pltpu.roll(bm, nrows - 1, 0) + b1_ref[...])
    comp = _dot(hid.astype(BF16), w2_ref[...])
    is_key = pl.program_id(0) == 0
    c = jnp.where(is_key, c_ref[...], 1.0)
    s = jnp.where(is_key, s_ref[...], 0.0)
    o_ref[...] = _partial_rotary(comp, c, s, half).astype(o_ref.dtype)


def _cmp_mlp(raw, cmp_pos, w1, b1, w2, c_tab, s_tab, bsz, t_len, half):
    g, hd = NSA_KV_GROUPS, LANES
    nrows = t_len // CMP_STRIDE
    feat = CMP_STRIDE * hd
    hidden = w1.shape[-1]
    pos = cmp_pos.reshape(2, CMP_BLOCK // CMP_STRIDE, feat)
    return pl.pallas_call(
        functools.partial(_cmp_mlp_kernel, half=half),
        grid=(2, g, bsz),
        in_specs=[
            pl.BlockSpec((None, None, t_len, hd), lambda c, gg, b: (c, gg, b, 0)),
            pl.BlockSpec((None, 2, feat), lambda c, gg, b: (c, 0, 0)),
            pl.BlockSpec((None, feat, hidden), lambda c, gg, b: (c, 0, 0)),
            pl.BlockSpec((None, feat, hidden), lambda c, gg, b: (c, 1, 0)),
            pl.BlockSpec((None, 1, hidden), lambda c, gg, b: (c, 0, 0)),
            pl.BlockSpec((None, hidden, hd), lambda c, gg, b: (c, 0, 0)),
            pl.BlockSpec((nrows, hd), lambda c, gg, b: (0, 0)),
            pl.BlockSpec((nrows, hd), lambda c, gg, b: (0, 0)),
        ],
        out_specs=pl.BlockSpec((None, None, None, nrows, hd), lambda c, gg, b: (c, gg, b, 0, 0)),
        out_shape=jax.ShapeDtypeStruct((2, g, bsz, nrows, hd), BF16),
        compiler_params=_params("arbitrary", "arbitrary", "arbitrary"),
        name="cmp_mlp",
    )(raw, pos, w1, w1, b1.reshape(2, 1, hidden), w2, c_tab, s_tab)


def _with_ones(v):
    return jnp.concatenate([v, jnp.ones_like(v)], axis=1)


def _nsa_kernel(q_ref, gate_ref, kc_ref, vc_ref, ks_ref, vs_ref, kw_ref, vw_ref, ovt_ref, ex_ref,
                o_ref, mask_ref, *, tq, tc, hg, hd, max_chunks):
    qi = pl.program_id(2)
    t0 = qi * tq
    rows = hg * tq
    q = q_ref[...]
    q4 = jnp.concatenate([q[:, i * hd:(i + 1) * hd] for i in range(hg)], axis=0)
    tpos = t0 + lax.broadcasted_iota(jnp.int32, (tq, 1), 0)

    ncmp = kc_ref.shape[0]
    s3 = _dot_nt(q4, kc_ref[...]).reshape(hg, tq, ncmp)
    cmp_end = lax.broadcasted_iota(jnp.int32, (tq, ncmp), 1) * CMP_STRIDE + (CMP_BLOCK - 1)
    s3 = jnp.where((cmp_end <= tpos)[None], s3, NEG_INF)
    e3 = jnp.exp2(s3 - jnp.max(s3, axis=-1, keepdims=True))
    has_cmp = (tpos >= CMP_BLOCK - 1).astype(F32)
    p3 = (e3 / jnp.sum(e3, axis=-1, keepdims=True) * has_cmp[None]).astype(BF16)
    o_cmp = _dot(p3.reshape(rows, ncmp), vc_ref[...])

    nslc = ovt_ref.shape[0]
    p_slc = _dot_nt(ovt_ref[...], jnp.concatenate([p3[i] for i in range(hg)], axis=1))
    trow = t0 + lax.broadcasted_iota(jnp.int32, (1, tq), 1)
    blk = lax.broadcasted_iota(jnp.int32, (nslc, tq), 0)
    cur = jnp.right_shift(trow, SLC_BLOCK.bit_length() - 1)
    valid = blk * SLC_BLOCK <= trow
    forced = (blk == 0) | (blk == cur) | (blk == cur - 1)
    score = jnp.where(valid, p_slc + jnp.where(forced, FORCE_BONUS, 0.0), -1.0)
    rank = jnp.zeros((nslc, tq), jnp.int32)
    for kb in range(nslc):
        ck = score[kb:kb + 1, :]
        rank += ((ck > score) | ((ck == score) & (blk > kb))).astype(jnp.int32)
    sel = jnp.where(rank < SLC_TOP, 1.0, 0.0).astype(BF16)
    mask_ref[...] = _dot_tn(sel, ex_ref[...])

    gts = gate_ref[...]
    lane = lax.broadcasted_iota(jnp.int32, (1, gts.shape[1]), 1)
    gate0 = pl.program_id(1) * (3 * hg)
    gate_col = [jnp.sum(jnp.where(lane == gate0 + c, gts, 0.0), axis=1, keepdims=True) for c in range(3 * hg)]

    def window_and_selected(n_chunks):
        wlen = WINDOW + tq
        w0 = pl.multiple_of(jnp.maximum(qi - WINDOW // tq, 0) * tq, tq)
        sw = _dot_nt(q4, kw_ref[pl.ds(w0, wlen), :]).reshape(hg, tq, wlen)
        kpos = w0 + lax.broadcasted_iota(jnp.int32, (tq, wlen), 1)
        okw = (kpos <= tpos) & (kpos > tpos - WINDOW)
        sw = jnp.where(okw[None], sw, NEG_INF)
        ew = jnp.exp2(sw - jnp.max(sw, axis=-1, keepdims=True)).astype(BF16)
        rw = _dot(ew.reshape(rows, wlen), _with_ones(vw_ref[pl.ds(w0, wlen), :]))
        o_win = rw[:, :hd] / rw[:, hd:]

        m_i = acc = None
        for c in range(n_chunks):
            k0 = c * tc
            s = _dot_nt(q4, ks_ref[k0:k0 + tc, :]).reshape(hg, tq, tc)
            ok = mask_ref[:, k0:k0 + tc] > 0.5
            if c == n_chunks - 1:
                ok &= (k0 + lax.broadcasted_iota(jnp.int32, (tq, tc), 1)) <= tpos
            s = jnp.where(ok[None], s, NEG_INF)
            m_c = jnp.max(s, axis=-1, keepdims=True)
            m_new = m_c if c == 0 else jnp.maximum(m_i, m_c)
            pv = _dot(jnp.exp2(s - m_new).astype(BF16).reshape(rows, tc), _with_ones(vs_ref[k0:k0 + tc, :]))
            acc = pv if c == 0 else acc * jnp.exp2(m_i - m_new).reshape(rows, 1) + pv
            m_i = m_new
        o_slc = acc[:, :hd] / acc[:, hd:]

        for i in range(hg):
            r = slice(i * tq, (i + 1) * tq)
            o = gate_col[3 * i] * o_cmp[r] + gate_col[3 * i + 1] * o_slc[r] + gate_col[3 * i + 2] * o_win[r]
            o_ref[:, i * hd:(i + 1) * hd] = o.astype(o_ref.dtype)

    n_chunks = lax.div(t0, tc) + 1
    for n in range(1, max_chunks + 1):
        pl.when(n_chunks == n)(functools.partial(window_and_selected, n))


def _nsa_attention(q, gates, kv_cmp, kv_rest, bsz, t_len):
    g, hg, hd = NSA_KV_GROUPS, NSA_HEADS_PER_GROUP, LANES
    tq, tc = NSA_Q_TILE, NSA_KEY_CHUNK
    nq = t_len // tq
    ncmp_pad = t_len // CMP_STRIDE
    n_cmp = (t_len - CMP_BLOCK) // CMP_STRIDE + 1
    n_slc = t_len // SLC_BLOCK
    cs = np.arange(ncmp_pad)[None, :] * CMP_STRIDE
    ss = np.arange(n_slc)[:, None] * SLC_BLOCK
    ovt = ((cs < ss + SLC_BLOCK) & (cs + CMP_BLOCK > ss) & (np.arange(ncmp_pad)[None, :] < n_cmp)).astype(np.float32)
    ovt = jnp.asarray(np.tile(ovt, (1, hg)), BF16)
    expand = jnp.asarray((np.arange(t_len)[None, :] // SLC_BLOCK == np.arange(n_slc)[:, None]).astype(np.float32), BF16)
    kv_spec = lambda slot: pl.BlockSpec((None, None, t_len, hd), lambda b, gg, qi: (slot, gg, b, 0))
    cmp_spec = lambda slot: pl.BlockSpec((None, None, None, ncmp_pad, hd), lambda b, gg, qi: (slot, gg, b, 0, 0))
    return pl.pallas_call(
        functools.partial(_nsa_kernel, tq=tq, tc=tc, hg=hg, hd=hd, max_chunks=t_len // tc),
        grid=(bsz, g, nq),
        in_specs=[
            pl.BlockSpec((tq, hg * hd), lambda b, gg, qi: (b * nq + qi, gg)),
            pl.BlockSpec((tq, LANES), lambda b, gg, qi: (b * nq + qi, 0)),
            cmp_spec(0), cmp_spec(1),
            kv_spec(0), kv_spec(1), kv_spec(2), kv_spec(3),
            pl.BlockSpec((n_slc, hg * ncmp_pad), lambda b, gg, qi: (0, 0)),
            pl.BlockSpec((n_slc, t_len), lambda b, gg, qi: (0, 0)),
        ],
        out_specs=pl.BlockSpec((tq, hg * hd), lambda b, gg, qi: (b * nq + qi, gg)),
        out_shape=jax.ShapeDtypeStruct((bsz * t_len, g * hg * hd), BF16),
        scratch_shapes=[pltpu.VMEM((tq, t_len), F32)],
        compiler_params=_params("parallel", "parallel", "arbitrary"),
        name="nsa_attention",
    )(q, gates, kv_cmp, kv_cmp, kv_rest, kv_rest, kv_rest, kv_rest, ovt, expand)


def _rope_tables(pos, rot_dims, theta):
    half = rot_dims // 2
    inv_freq = jnp.power(jnp.float32(theta), -jnp.arange(half, dtype=F32) / half)
    ang = pos.astype(F32)[:, None] * inv_freq[None, :]
    cos, sin = jnp.cos(ang), jnp.sin(ang)
    n = pos.shape[0]
    pad = LANES - rot_dims
    c_tab = jnp.concatenate([cos, cos, jnp.ones((n, pad), F32)], axis=1)
    s_tab = jnp.concatenate([-sin, sin, jnp.zeros((n, pad), F32)], axis=1)
    return c_tab, s_tab


def kernel(x, ln_g, ln_b, ffn1_w_in, ffn1_w_out, ffn2_w_in, ffn2_w_out, ret_w_in, ret_gn_g, ret_gn_b, ret_w_out, kv_w, cmp_pos, cmp_w1, cmp_b1, cmp_w2, nsa_w_q, nsa_w_out):
    bsz, t_len, d = x.shape
    m = bsz * t_len
    bf = lambda w: w.astype(BF16)
    h = x.reshape(m, d)
    pos = jnp.arange(t_len)

    h, hb, w_in_next, w_out_next = _ffn(h, bf(ffn1_w_in[0]), bf(ffn1_w_out[0]), ln_g[0, 0], ln_b[0, 0],
                                        cast_next=(ffn2_w_in, ffn2_w_out, 0), emit_bf16=True)
    dk = d // RET_HEADS
    dv = 2 * d // RET_HEADS
    half = dk // 2
    inv_freq = jnp.power(jnp.float32(RET_ROPE_BASE), -jnp.arange(half, dtype=F32) / half)
    ang = pos.astype(F32)[:, None] * inv_freq[None, :]
    n_ret = ret_w_in.shape[-1]
    ret_epi = functools.partial(_ret_epilogue, q_tiles=RET_HEADS * dk // 1024,
                                gate_tile0=(2 * RET_HEADS * dk + RET_HEADS * dv) // 1024, k_scale=dk ** -0.5, half=half)
    proj = _wproj(hb, ret_w_in, col0=0, ncols=n_ret, epilogue=ret_epi,
                  out_spec=lambda tm, tn: pl.BlockSpec((tm, tn), lambda j, i: (i, j)),
                  out_shape=jax.ShapeDtypeStruct((m, n_ret), BF16),
                  extra=(jnp.cos(ang), jnp.sin(ang)), extra_specs=_table_specs(t_len, half), name="ret_proj")
    y = _retention(proj, ret_gn_g[0], ret_gn_b[0], bsz, t_len, dk, dv)
    h = _out_proj(y, bf(ret_w_out), h, ln_g[0, 1], ln_b[0, 1], "ret_out_proj")
    h, hb, w_in_next, w_out_next = _ffn(h, w_in_next, w_out_next, ln_g[0, 2], ln_b[0, 2],
                                        cast_next=(ffn1_w_in, ffn1_w_out, 1), emit_bf16=True)

    hd = d // NSA_HEADS
    rope_dims = hd // 4
    rope_half = rope_dims // 2
    c_tab, s_tab = _rope_tables(pos, rope_dims, ROPE_THETA)
    w_kv = kv_w[None]
    slot_cols = NSA_KV_GROUPS * hd
    raw_cmp = _head_proj(hb, w_kv, c_tab, s_tab, t_len, col0=0, nslots=2, out_dtype=F32, half=rope_half,
                         scale=1.0, rotate="none", split=True, name="kv_cmp_proj")
    kv_rest = _head_proj(hb, w_kv, c_tab, s_tab, t_len, col0=2 * slot_cols, nslots=4, out_dtype=BF16,
                         half=rope_half, scale=1.0, rotate="even", split=True, name="kv_slc_win_proj")
    n_rows = t_len // CMP_STRIDE
    cmp_end = jnp.arange(n_rows) * CMP_STRIDE + CMP_BLOCK - 1
    cc_tab, cs_tab = _rope_tables(cmp_end, rope_dims, ROPE_THETA)
    kv_cmp = _cmp_mlp(raw_cmp, cmp_pos, bf(cmp_w1), cmp_b1, bf(cmp_w2), cc_tab, cs_tab, bsz, t_len, rope_half)

    h, hb, w_in_next, w_out_next = _ffn(h, w_in_next, w_out_next, ln_g[1, 0], ln_b[1, 0],
                                        cast_next=(ffn2_w_in, ffn2_w_out, 1), emit_bf16=True)
    n_q = NSA_HEADS * hd
    q = _head_proj(hb, nsa_w_q, c_tab, s_tab, t_len, col0=0, nslots=n_q // slot_cols, out_dtype=BF16,
                   half=rope_half, scale=hd ** -0.5 * LOG2_E, rotate="all", split=False, name="nsa_q_proj")
    n_gate = NSA_HEADS * 3
    w_gate = jnp.pad(nsa_w_q[0][:, n_q:], ((0, 0), (0, LANES - n_gate)))[None]
    gates = _wproj(hb, w_gate, col0=0, ncols=LANES, epilogue=_sigmoid_epilogue,
                   out_spec=lambda tm, tn: pl.BlockSpec((tm, tn), lambda j, i: (i, j)),
                   out_shape=jax.ShapeDtypeStruct((m, LANES), F32), name="nsa_gate_proj")
    o = _nsa_attention(q, gates, kv_cmp, kv_rest, bsz, t_len)
    h = _out_proj(o, bf(nsa_w_out), h, ln_g[1, 1], ln_b[1, 1], "nsa_out_proj")
    (h,) = _ffn(h, w_in_next, w_out_next, ln_g[1, 2], ln_b[1, 2])
    return h.reshape(bsz, t_len, d)
```

```python
import functools
import math

import numpy as np
import jax
import jax.numpy as jnp
from jax import lax
from jax.experimental import pallas as pl
from jax.experimental.pallas import tpu as pltpu

F32 = jnp.float32
BF16 = jnp.bfloat16

DEPTH = 2
N_A_LAYERS = DEPTH // 2
DEEPNORM_ALPHA = (2 * DEPTH) ** 0.25
LN_EPS = 1e-5
MACARON_WEIGHT = 0.5

RET_HEADS = 8
RET_CHUNK = 256
RET_ROPE_BASE = 10000.0

NSA_HEADS = 16
NSA_KV_GROUPS = 4
NSA_HEADS_PER_GROUP = NSA_HEADS // NSA_KV_GROUPS
CMP_BLOCK = 32
CMP_STRIDE = 16
SLC_BLOCK = 64
SLC_TOP = 8
WINDOW = 512
FORCE_BONUS = 1e4
ROPE_THETA = 500000.0
NEG_INF = -1e30
LOG2_E = math.log2(math.e)

V7X_VMEM_BYTES = 64 * 1024 * 1024
VMEM_LIMIT = V7X_VMEM_BYTES - 8 * 1024 * 1024
LANES = 128

FFN_SUBTILES = 2
NSA_Q_TILE = 256
NSA_KEY_CHUNK = 512


def _params(*semantics):
    return pltpu.CompilerParams(dimension_semantics=semantics, vmem_limit_bytes=VMEM_LIMIT)


def _layer_norm(y, g, b):
    mu = jnp.mean(y, axis=-1, keepdims=True)
    d = y - mu
    var = jnp.mean(d * d, axis=-1, keepdims=True)
    return d * lax.rsqrt(var + LN_EPS) * g + b


def _dot(a, b):
    return jnp.dot(a, b, preferred_element_type=F32)


def _dot_nt(a, b):
    return lax.dot_general(a, b, (((1,), (1,)), ((), ())), preferred_element_type=F32)


def _dot_tn(a, b):
    return lax.dot_general(a, b, (((0,), (0,)), ((), ())), preferred_element_type=F32)


def _ffn_kernel(*refs, nf, cast_next, emit_bf16):
    refs = list(refs)
    x_ref, wa_ref, wu_ref, wo_ref, g_ref, b_ref = refs[:6]
    refs = refs[6:]
    if cast_next:
        nwi_ref, nwo_ref = refs[:2]
        refs = refs[2:]
    o_ref = refs.pop(0)
    ob_ref = refs.pop(0) if emit_bf16 else None
    xb_ref = refs.pop()
    if cast_next:
        nwi_out, nwo_out = refs
        nwi_out[...] = nwi_ref[...].astype(BF16)
        nwo_out[...] = nwo_ref[...].astype(BF16)
    f = pl.program_id(1)

    @pl.when(f == 0)
    def _():
        xb_ref[...] = x_ref[...].astype(BF16)
        o_ref[...] = jnp.zeros_like(o_ref)

    xb = xb_ref[...]
    sub = wa_ref.shape[1] // FFN_SUBTILES
    part = None
    for s in range(FFN_SUBTILES):
        a = _dot(xb, wa_ref[:, s * sub:(s + 1) * sub])
        u = _dot(xb, wu_ref[:, s * sub:(s + 1) * sub])
        h = (a * jax.nn.sigmoid(a)) * u
        p = _dot(h.astype(BF16), wo_ref[s * sub:(s + 1) * sub, :])
        part = p if part is None else part + p
    o_ref[...] += part

    @pl.when(f == nf - 1)
    def _():
        y = DEEPNORM_ALPHA * x_ref[...] + MACARON_WEIGHT * o_ref[...]
        out = _layer_norm(y, g_ref[...], b_ref[...])
        o_ref[...] = out
        if emit_bf16:
            ob_ref[...] = out.astype(BF16)


def _ffn(x, w_in, w_out, g, b, cast_next=None, emit_bf16=False):
    m, d = x.shape
    f_dim = w_out.shape[0]
    tm, tf = 512, 512
    ni, nf = m // tm, f_dim // tf
    row_spec = pl.BlockSpec((tm, d), lambda i, f: (i, 0))
    vec_spec = pl.BlockSpec((1, d), lambda i, f: (0, 0))
    in_specs = [row_spec, pl.BlockSpec((d, tf), lambda i, f: (0, f)), pl.BlockSpec((d, tf), lambda i, f: (0, f + nf)),
                pl.BlockSpec((tf, d), lambda i, f: (f, 0)), vec_spec, vec_spec]
    args = [x, w_in, w_in, w_out, g.reshape(1, d), b.reshape(1, d)]
    out_specs, out_shape = [row_spec], [jax.ShapeDtypeStruct((m, d), F32)]
    if emit_bf16:
        out_specs.append(row_spec)
        out_shape.append(jax.ShapeDtypeStruct((m, d), BF16))
    if cast_next is not None:
        nw_in, nw_out, layer = cast_next
        ri, ci = d // ni, 2 * f_dim // nf
        ro = f_dim // (ni * nf)
        assert ri % 16 == 0 and ci % LANES == 0 and ro % 16 == 0
        in_specs += [pl.BlockSpec((None, ri, ci), lambda i, f: (layer, i, f)),
                     pl.BlockSpec((None, ro, d), lambda i, f: (layer, i * nf + f, 0))]
        out_specs += [pl.BlockSpec((ri, ci), lambda i, f: (i, f)), pl.BlockSpec((ro, d), lambda i, f: (i * nf + f, 0))]
        out_shape += [jax.ShapeDtypeStruct((d, 2 * f_dim), BF16), jax.ShapeDtypeStruct((f_dim, d), BF16)]
        args += [nw_in, nw_out]
    return pl.pallas_call(
        functools.partial(_ffn_kernel, nf=nf, cast_next=cast_next is not None, emit_bf16=emit_bf16),
        grid=(ni, nf),
        in_specs=in_specs,
        out_specs=out_specs,
        out_shape=out_shape,
        scratch_shapes=[pltpu.VMEM((tm, d), BF16)],
        compiler_params=_params("arbitrary", "arbitrary"),
        name="ffn_deepnorm",
    )(*args)


def _wproj_kernel(*refs, n_extra, epilogue):
    xb_ref, w_ref = refs[:2]
    extra = refs[2:2 + n_extra]
    o_ref, wb_ref = refs[2 + n_extra:]

    @pl.when(pl.program_id(1) == 0)
    def _():
        wb_ref[...] = w_ref[...].astype(BF16)

    epilogue(_dot(xb_ref[...], wb_ref[...]), pl.program_id(0), extra, o_ref)


def _wproj(xb, w, *, col0, ncols, epilogue, out_spec, out_shape, extra=(), extra_specs=(), name):
    m, k = xb.shape
    tm, tn = 1024, min(1024, ncols)
    assert col0 % tn == 0 and ncols % tn == 0
    return pl.pallas_call(
        functools.partial(_wproj_kernel, n_extra=len(extra), epilogue=epilogue),
        grid=(ncols // tn, m // tm),
        in_specs=[pl.BlockSpec((tm, k), lambda j, i: (i, 0)),
                  pl.BlockSpec((None, k, tn), lambda j, i: (0, 0, j + col0 // tn)), *extra_specs],
        out_specs=out_spec(tm, tn),
        out_shape=out_shape,
        scratch_shapes=[pltpu.VMEM((k, tn), BF16)],
        compiler_params=_params("arbitrary", "arbitrary"),
        name=name,
    )(xb, w, *extra)


def _table_specs(t_len, width, tm=1024):
    tpb = t_len // tm
    return [pl.BlockSpec((tm, width), lambda j, i: (i % tpb, 0))] * 2


def _qk_rot_epilogue(acc, j, extra, o_ref, *, q_tiles, k_scale, half):
    cos_ref, sin_ref = extra
    scale = jnp.where(j >= q_tiles, k_scale, 1.0).astype(F32)
    cos = cos_ref[...]
    sin = sin_ref[...]
    for h in range(acc.shape[1] // (2 * half)):
        c0 = h * 2 * half
        x1 = acc[:, c0:c0 + half]
        x2 = acc[:, c0 + half:c0 + 2 * half]
        o_ref[:, c0:c0 + half] = ((x1 * cos - x2 * sin) * scale).astype(o_ref.dtype)
        o_ref[:, c0 + half:c0 + 2 * half] = ((x2 * cos + x1 * sin) * scale).astype(o_ref.dtype)


def _store_epilogue(acc, j, extra, o_ref):
    o_ref[...] = acc.astype(o_ref.dtype)


def _partial_rotary(x, c, s, half):
    lane = lax.broadcasted_iota(jnp.int32, x.shape, 1)
    partner = jnp.where(lane < half, pltpu.roll(x, LANES - half, 1), pltpu.roll(x, half, 1))
    return x * c + partner * s


def _heads_epilogue(acc, j, extra, o_ref, *, hd, hps, half, scale, rotate, split):
    c_ref, s_ref = extra
    for h in range(acc.shape[1] // hd):
        piece = acc[:, h * hd:(h + 1) * hd]
        if rotate == "all" or (rotate == "even" and h < hps):
            piece = _partial_rotary(piece, c_ref[...], s_ref[...], half)
        if scale != 1.0:
            piece = piece * scale
        if split:
            o_ref[h // hps, h % hps] = piece.astype(o_ref.dtype)
        else:
            o_ref[:, h * hd:(h + 1) * hd] = piece.astype(o_ref.dtype)


def _sigmoid_epilogue(acc, j, extra, o_ref):
    o_ref[...] = jax.nn.sigmoid(acc).astype(o_ref.dtype)


def _head_proj(xb, w, c_tab, s_tab, t_len, *, col0, nslots, out_dtype, half, scale, rotate, split, name):
    m = xb.shape[0]
    hd, hps = LANES, NSA_KV_GROUPS
    assert nslots % 2 == 0
    if split:
        out_spec = lambda tm, tn: pl.BlockSpec((2, hps, tm, hd), lambda j, i: (j, 0, i, 0))
        out_shape = jax.ShapeDtypeStruct((nslots, hps, m, hd), out_dtype)
    else:
        out_spec = lambda tm, tn: pl.BlockSpec((tm, tn), lambda j, i: (i, j))
        out_shape = jax.ShapeDtypeStruct((m, nslots * hps * hd), out_dtype)
    epi = functools.partial(_heads_epilogue, hd=hd, hps=hps, half=half, scale=scale, rotate=rotate, split=split)
    return _wproj(xb, w, col0=col0, ncols=nslots * hps * hd, epilogue=epi, out_spec=out_spec, out_shape=out_shape,
                  extra=(c_tab, s_tab), extra_specs=_table_specs(t_len, hd), name=name)


def _out_proj_kernel(y_ref, w_ref, x_ref, g_ref, b_ref, o_ref, *, nsub):
    rs = y_ref.shape[0] // nsub
    for s in range(nsub):
        r = slice(s * rs, (s + 1) * rs)
        acc = _dot(y_ref[r, :], w_ref[...])
        o_ref[r, :] = _layer_norm(DEEPNORM_ALPHA * x_ref[r, :] + acc, g_ref[...], b_ref[...])


def _out_proj(y, w, x, g, b, name):
    m, k = y.shape
    d = w.shape[2]
    tm = 512
    row_spec = pl.BlockSpec((tm, d), lambda i: (i, 0))
    return pl.pallas_call(
        functools.partial(_out_proj_kernel, nsub=2),
        grid=(m // tm,),
        in_specs=[
            pl.BlockSpec((tm, k), lambda i: (i, 0)),
            pl.BlockSpec((None, k, d), lambda i: (0, 0, 0), pipeline_mode=pl.Buffered(1)),
            row_spec,
            pl.BlockSpec((1, d), lambda i: (0, 0)),
            pl.BlockSpec((1, d), lambda i: (0, 0)),
        ],
        out_specs=row_spec,
        out_shape=jax.ShapeDtypeStruct((m, d), F32),
        compiler_params=_params("parallel"),
        name=name,
    )(y, w, x, g.reshape(1, d), b.reshape(1, d))


def _retention_kernel(gam_ref, q_ref, k_ref, v_ref, g_ref, dec_ref, xi_ref, zeta_ref, gng_ref, gnb_ref,
                      o_ref, state_ref, *, nh, dk, dv):
    @pl.when(pl.program_id(1) == 0)
    def _():
        state_ref[...] = jnp.zeros_like(state_ref)

    for h in range(nh):
        q = q_ref[:, h * dk:(h + 1) * dk]
        k = k_ref[:, h * dk:(h + 1) * dk]
        v = v_ref[:, h * dv:(h + 1) * dv]
        state = state_ref[h]
        s = _dot_nt(q, k) * dec_ref[h]
        out = _dot(s.astype(BF16), v) + _dot(q, state.astype(BF16)) * xi_ref[h]
        kz = (k.astype(F32) * zeta_ref[h]).astype(BF16)
        state_ref[h] = _dot_tn(kz, v) + gam_ref[h] * state

        mu = jnp.mean(out, axis=-1, keepdims=True)
        d = out - mu
        var = jnp.mean(d * d, axis=-1, keepdims=True)
        cols = slice(h * dv, (h + 1) * dv)
        gn = d * lax.rsqrt(var + LN_EPS) * gng_ref[:, cols] + gnb_ref[:, cols]
        gate = g_ref[:, cols].astype(F32)
        o_ref[:, cols] = ((gate * jax.nn.sigmoid(gate)) * gn).astype(o_ref.dtype)


def _retention(qk, vg, gn_g, gn_b, bsz, t_len, dk, dv):
    h = RET_HEADS
    c = RET_CHUNK
    nc = t_len // c
    m = bsz * t_len
    log_gamma = jnp.log(1.0 - jnp.power(2.0, -5.0 - jnp.arange(h, dtype=F32)))
    idx = jnp.arange(c, dtype=F32)
    diff = idx[:, None] - idx[None, :]
    decay = jnp.where(diff >= 0, jnp.exp(log_gamma[:, None, None] * jnp.maximum(diff, 0.0)), 0.0)
    xi = jnp.exp(log_gamma[:, None] * (idx + 1.0))[..., None]
    zeta = jnp.exp(log_gamma[:, None] * (c - 1.0 - idx))[..., None]
    gamma_c = jnp.exp(log_gamma * c)
    row = lambda b, cc: b * nc + cc
    whole = lambda shape: pl.BlockSpec(shape, lambda b, cc: (0,) * len(shape))
    return pl.pallas_call(
        functools.partial(_retention_kernel, nh=h, dk=dk, dv=dv),
        grid=(bsz, nc),
        in_specs=[
            pl.BlockSpec(memory_space=pltpu.SMEM),
            pl.BlockSpec((c, h * dk), lambda b, cc: (row(b, cc), 0)),
            pl.BlockSpec((c, h * dk), lambda b, cc: (row(b, cc), 1)),
            pl.BlockSpec((c, h * dv), lambda b, cc: (row(b, cc), 0)),
            pl.BlockSpec((c, h * dv), lambda b, cc: (row(b, cc), 1)),
            whole((h, c, c)), whole((h, c, 1)), whole((h, c, 1)),
            whole((1, h * dv)), whole((1, h * dv)),
        ],
        out_specs=pl.BlockSpec((c, h * dv), lambda b, cc: (row(b, cc), 0)),
        out_shape=jax.ShapeDtypeStruct((m, h * dv), BF16),
        scratch_shapes=[pltpu.VMEM((h, dk, dv), F32)],
        compiler_params=_params("parallel", "arbitrary"),
        name="retention_chunks",
    )(gamma_c, qk, qk, vg, vg, decay, xi, zeta, gn_g.reshape(1, h * dv), gn_b.reshape(1, h * dv))


def _cmp_mlp_kernel(x_ref, pos_ref, w1a_ref, w1b_ref, b1_ref, w2_ref, c_ref, s_ref, o_ref, *, half):
    nrows = o_ref.shape[0]
    x = jnp.concatenate([x_ref[pl.ds(l, nrows, stride=CMP_STRIDE), :] for l in range(CMP_STRIDE)], axis=1)
    pos = pos_ref[...]
    a = _dot((x + pos[0:1, :]).astype(BF16), w1a_ref[...])
    bm = _dot((x + pos[1:2, :]).astype(BF16), w1b_ref[...])
    hid = jax.nn.gelu(a + pltpu.roll(bm, nrows - 1, 0) + b1_ref[...])
    comp = _dot(hid.astype(BF16), w2_ref[...])
    is_key = pl.program_id(0) == 0
    c = jnp.where(is_key, c_ref[...], 1.0)
    s = jnp.where(is_key, s_ref[...], 0.0)
    o_ref[...] = _partial_rotary(comp, c, s, half).astype(o_ref.dtype)


def _cmp_mlp(raw, cmp_pos, w1, b1, w2, c_tab, s_tab, bsz, t_len, half):
    g, hd = NSA_KV_GROUPS, LANES
    nrows = t_len // CMP_STRIDE
    feat = CMP_STRIDE * hd
    hidden = w1.shape[-1]
    pos = cmp_pos.reshape(2, CMP_BLOCK // CMP_STRIDE, feat)
    return pl.pallas_call(
        functools.partial(_cmp_mlp_kernel, half=half),
        grid=(2, g, bsz),
        in_specs=[
            pl.BlockSpec((None, None, t_len, hd), lambda c, gg, b: (c, gg, b, 0)),
            pl.BlockSpec((None, 2, feat), lambda c, gg, b: (c, 0, 0)),
            pl.BlockSpec((None, feat, hidden), lambda c, gg, b: (c, 0, 0)),
            pl.BlockSpec((None, feat, hidden), lambda c, gg, b: (c, 1, 0)),
            pl.BlockSpec((None, 1, hidden), lambda c, gg, b: (c, 0, 0)),
            pl.BlockSpec((None, hidden, hd), lambda c, gg, b: (c, 0, 0)),
            pl.BlockSpec((nrows, hd), lambda c, gg, b: (0, 0)),
            pl.BlockSpec((nrows, hd), lambda c, gg, b: (0, 0)),
        ],
        out_specs=pl.BlockSpec((None, None, None, nrows, hd), lambda c, gg, b: (c, gg, b, 0, 0)),
        out_shape=jax.ShapeDtypeStruct((2, g, bsz, nrows, hd), BF16),
        compiler_params=_params("arbitrary", "arbitrary", "arbitrary"),
        name="cmp_mlp",
    )(raw, pos, w1, w1, b1.reshape(2, 1, hidden), w2, c_tab, s_tab)


def _with_ones(v):
    return jnp.concatenate([v, jnp.ones_like(v)], axis=1)


def _nsa_kernel(q_ref, gate_ref, kc_ref, vc_ref, ks_ref, vs_ref, kw_ref, vw_ref, ovt_ref, ex_ref,
                o_ref, mask_ref, *, tq, tc, hg, hd, max_chunks):
    qi = pl.program_id(2)
    t0 = qi * tq
    rows = hg * tq
    q = q_ref[...]
    q4 = jnp.concatenate([q[:, i * hd:(i + 1) * hd] for i in range(hg)], axis=0)
    tpos = t0 + lax.broadcasted_iota(jnp.int32, (tq, 1), 0)

    ncmp = kc_ref.shape[0]
    s3 = _dot_nt(q4, kc_ref[...]).reshape(hg, tq, ncmp)
    cmp_end = lax.broadcasted_iota(jnp.int32, (tq, ncmp), 1) * CMP_STRIDE + (CMP_BLOCK - 1)
    s3 = jnp.where((cmp_end <= tpos)[None], s3, NEG_INF)
    e3 = jnp.exp2(s3 - jnp.max(s3, axis=-1, keepdims=True))
    has_cmp = (tpos >= CMP_BLOCK - 1).astype(F32)
    p3 = (e3 / jnp.sum(e3, axis=-1, keepdims=True) * has_cmp[None]).astype(BF16)
    o_cmp = _dot(p3.reshape(rows, ncmp), vc_ref[...])

    nslc = ovt_ref.shape[0]
    p_slc = _dot_nt(ovt_ref[...], jnp.concatenate([p3[i] for i in range(hg)], axis=1))
    trow = t0 + lax.broadcasted_iota(jnp.int32, (1, tq), 1)
    blk = lax.broadcasted_iota(jnp.int32, (nslc, tq), 0)
    cur = jnp.right_shift(trow, SLC_BLOCK.bit_length() - 1)
    valid = blk * SLC_BLOCK <= trow
    forced = (blk == 0) | (blk == cur) | (blk == cur - 1)
    score = jnp.where(valid, p_slc + jnp.where(forced, FORCE_BONUS, 0.0), -1.0)
    rank = jnp.zeros((nslc, tq), jnp.int32)
    for kb in range(nslc):
        ck = score[kb:kb + 1, :]
        rank += ((ck > score) | ((ck == score) & (blk > kb))).astype(jnp.int32)
    sel = jnp.where(rank < SLC_TOP, 1.0, 0.0).astype(BF16)
    mask_ref[...] = _dot_tn(sel, ex_ref[...])

    gts = gate_ref[...]
    lane = lax.broadcasted_iota(jnp.int32, (1, gts.shape[1]), 1)
    gate0 = pl.program_id(1) * (3 * hg)
    gate_col = [jnp.sum(jnp.where(lane == gate0 + c, gts, 0.0), axis=1, keepdims=True) for c in range(3 * hg)]

    def window_and_selected(n_chunks):
        wlen = WINDOW + tq
        w0 = pl.multiple_of(jnp.maximum(qi - WINDOW // tq, 0) * tq, tq)
        sw = _dot_nt(q4, kw_ref[pl.ds(w0, wlen), :]).reshape(hg, tq, wlen)
        kpos = w0 + lax.broadcasted_iota(jnp.int32, (tq, wlen), 1)
        okw = (kpos <= tpos) & (kpos > tpos - WINDOW)
        sw = jnp.where(okw[None], sw, NEG_INF)
        ew = jnp.exp2(sw - jnp.max(sw, axis=-1, keepdims=True)).astype(BF16)
        rw = _dot(ew.reshape(rows, wlen), _with_ones(vw_ref[pl.ds(w0, wlen), :]))
        o_win = rw[:, :hd] / rw[:, hd:]

        m_i = acc = None
        for c in range(n_chunks):
            k0 = c * tc
            s = _dot_nt(q4, ks_ref[k0:k0 + tc, :]).reshape(hg, tq, tc)
            ok = mask_ref[:, k0:k0 + tc] > 0.5
            if c == n_chunks - 1:
                ok &= (k0 + lax.broadcasted_iota(jnp.int32, (tq, tc), 1)) <= tpos
            s = jnp.where(ok[None], s, NEG_INF)
            m_c = jnp.max(s, axis=-1, keepdims=True)
            m_new = m_c if c == 0 else jnp.maximum(m_i, m_c)
            pv = _dot(jnp.exp2(s - m_new).astype(BF16).reshape(rows, tc), _with_ones(vs_ref[k0:k0 + tc, :]))
            acc = pv if c == 0 else acc * jnp.exp2(m_i - m_new).reshape(rows, 1) + pv
            m_i = m_new
        o_slc = acc[:, :hd] / acc[:, hd:]

        for i in range(hg):
            r = slice(i * tq, (i + 1) * tq)
            o = gate_col[3 * i] * o_cmp[r] + gate_col[3 * i + 1] * o_slc[r] + gate_col[3 * i + 2] * o_win[r]
            o_ref[:, i * hd:(i + 1) * hd] = o.astype(o_ref.dtype)

    n_chunks = lax.div(t0, tc) + 1
    for n in range(1, max_chunks + 1):
        pl.when(n_chunks == n)(functools.partial(window_and_selected, n))


def _nsa_attention(q, gates, kv_cmp, kv_rest, bsz, t_len):
    g, hg, hd = NSA_KV_GROUPS, NSA_HEADS_PER_GROUP, LANES
    tq, tc = NSA_Q_TILE, NSA_KEY_CHUNK
    nq = t_len // tq
    ncmp_pad = t_len // CMP_STRIDE
    n_cmp = (t_len - CMP_BLOCK) // CMP_STRIDE + 1
    n_slc = t_len // SLC_BLOCK
    cs = np.arange(ncmp_pad)[None, :] * CMP_STRIDE
    ss = np.arange(n_slc)[:, None] * SLC_BLOCK
    ovt = ((cs < ss + SLC_BLOCK) & (cs + CMP_BLOCK > ss) & (np.arange(ncmp_pad)[None, :] < n_cmp)).astype(np.float32)
    ovt = jnp.asarray(np.tile(ovt, (1, hg)), BF16)
    expand = jnp.asarray((np.arange(t_len)[None, :] // SLC_BLOCK == np.arange(n_slc)[:, None]).astype(np.float32), BF16)
    kv_spec = lambda slot: pl.BlockSpec((None, None, t_len, hd), lambda b, gg, qi: (slot, gg, b, 0))
    cmp_spec = lambda slot: pl.BlockSpec((None, None, None, ncmp_pad, hd), lambda b, gg, qi: (slot, gg, b, 0, 0))
    return pl.pallas_call(
        functools.partial(_nsa_kernel, tq=tq, tc=tc, hg=hg, hd=hd, max_chunks=t_len // tc),
        grid=(bsz, g, nq),
        in_specs=[
            pl.BlockSpec((tq, hg * hd), lambda b, gg, qi: (b * nq + qi, gg)),
            pl.BlockSpec((tq, LANES), lambda b, gg, qi: (b * nq + qi, 0)),
            cmp_spec(0), cmp_spec(1),
            kv_spec(0), kv_spec(1), kv_spec(2), kv_spec(3),
            pl.BlockSpec((n_slc, hg * ncmp_pad), lambda b, gg, qi: (0, 0)),
            pl.BlockSpec((n_slc, t_len), lambda b, gg, qi: (0, 0)),
        ],
        out_specs=pl.BlockSpec((tq, hg * hd), lambda b, gg, qi: (b * nq + qi, gg)),
        out_shape=jax.ShapeDtypeStruct((bsz * t_len, g * hg * hd), BF16),
        scratch_shapes=[pltpu.VMEM((tq, t_len), F32)],
        compiler_params=_params("parallel", "parallel", "arbitrary"),
        name="nsa_attention",
    )(q, gates, kv_cmp, kv_cmp, kv_rest, kv_rest, kv_rest, kv_rest, ovt, expand)


def _rope_tables(pos, rot_dims, theta):
    half = rot_dims // 2
    inv_freq = jnp.power(jnp.float32(theta), -jnp.arange(half, dtype=F32) / half)
    ang = pos.astype(F32)[:, None] * inv_freq[None, :]
    cos, sin = jnp.cos(ang), jnp.sin(ang)
    n = pos.shape[0]
    pad = LANES - rot_dims
    c_tab = jnp.concatenate([cos, cos, jnp.ones((n, pad), F32)], axis=1)
    s_tab = jnp.concatenate([-sin, sin, jnp.zeros((n, pad), F32)], axis=1)
    return c_tab, s_tab


def kernel(x, ln_g, ln_b, ffn1_w_in, ffn1_w_out, ffn2_w_in, ffn2_w_out, ret_w_in, ret_gn_g, ret_gn_b, ret_w_out, kv_w, cmp_pos, cmp_w1, cmp_b1, cmp_w2, nsa_w_q, nsa_w_out):
    bsz, t_len, d = x.shape
    m = bsz * t_len
    bf = lambda w: w.astype(BF16)
    h = x.reshape(m, d)
    pos = jnp.arange(t_len)

    h, hb, w_in_next, w_out_next = _ffn(h, bf(ffn1_w_in[0]), bf(ffn1_w_out[0]), ln_g[0, 0], ln_b[0, 0],
                                        cast_next=(ffn2_w_in, ffn2_w_out, 0), emit_bf16=True)
    dk = d // RET_HEADS
    dv = 2 * d // RET_HEADS
    half = dk // 2
    inv_freq = jnp.power(jnp.float32(RET_ROPE_BASE), -jnp.arange(half, dtype=F32) / half)
    ang = pos.astype(F32)[:, None] * inv_freq[None, :]
    n_qk, n_vg = 2 * RET_HEADS * dk, 2 * RET_HEADS * dv
    tile_spec = lambda tm, tn: pl.BlockSpec((tm, tn), lambda j, i: (i, j))
    rot_epi = functools.partial(_qk_rot_epilogue, q_tiles=RET_HEADS * dk // 1024, k_scale=dk ** -0.5, half=half)
    qk = _wproj(hb, ret_w_in, col0=0, ncols=n_qk, epilogue=rot_epi, out_spec=tile_spec,
                out_shape=jax.ShapeDtypeStruct((m, n_qk), BF16),
                extra=(jnp.cos(ang), jnp.sin(ang)), extra_specs=_table_specs(t_len, half), name="ret_qk_proj")
    vg = _wproj(hb, ret_w_in, col0=n_qk, ncols=n_vg, epilogue=_store_epilogue, out_spec=tile_spec,
                out_shape=jax.ShapeDtypeStruct((m, n_vg), BF16), name="ret_vg_proj")
    y = _retention(qk, vg, ret_gn_g[0], ret_gn_b[0], bsz, t_len, dk, dv)
    h = _out_proj(y, bf(ret_w_out), h, ln_g[0, 1], ln_b[0, 1], "ret_out_proj")
    h, hb, w_in_next, w_out_next = _ffn(h, w_in_next, w_out_next, ln_g[0, 2], ln_b[0, 2],
                                        cast_next=(ffn1_w_in, ffn1_w_out, 1), emit_bf16=True)

    hd = d // NSA_HEADS
    rope_dims = hd // 4
    rope_half = rope_dims // 2
    c_tab, s_tab = _rope_tables(pos, rope_dims, ROPE_THETA)
    w_kv = kv_w[None]
    slot_cols = NSA_KV_GROUPS * hd
    raw_cmp = _head_proj(hb, w_kv, c_tab, s_tab, t_len, col0=0, nslots=2, out_dtype=F32, half=rope_half,
                         scale=1.0, rotate="none", split=True, name="kv_cmp_proj")
    kv_rest = _head_proj(hb, w_kv, c_tab, s_tab, t_len, col0=2 * slot_cols, nslots=4, out_dtype=BF16,
                         half=rope_half, scale=1.0, rotate="even", split=True, name="kv_slc_win_proj")
    n_rows = t_len // CMP_STRIDE
    cmp_end = jnp.arange(n_rows) * CMP_STRIDE + CMP_BLOCK - 1
    cc_tab, cs_tab = _rope_tables(cmp_end, rope_dims, ROPE_THETA)
    kv_cmp = _cmp_mlp(raw_cmp, cmp_pos, bf(cmp_w1), cmp_b1, bf(cmp_w2), cc_tab, cs_tab, bsz, t_len, rope_half)

    h, hb, w_in_next, w_out_next = _ffn(h, w_in_next, w_out_next, ln_g[1, 0], ln_b[1, 0],
                                        cast_next=(ffn2_w_in, ffn2_w_out, 1), emit_bf16=True)
    n_q = NSA_HEADS * hd
    q = _head_proj(hb, nsa_w_q, c_tab, s_tab, t_len, col0=0, nslots=n_q // slot_cols, out_dtype=BF16,
                   half=rope_half, scale=hd ** -0.5 * LOG2_E, rotate="all", split=False, name="nsa_q_proj")
    n_gate = NSA_HEADS * 3
    w_gate = jnp.pad(nsa_w_q[0][:, n_q:], ((0, 0), (0, LANES - n_gate)))[None]
    gates = _wproj(hb, w_gate, col0=0, ncols=LANES, epilogue=_sigmoid_epilogue,
                   out_spec=lambda tm, tn: pl.BlockSpec((tm, tn), lambda j, i: (i, j)),
                   out_shape=jax.ShapeDtypeStruct((m, LANES), F32), name="nsa_gate_proj")
    o = _nsa_attention(q, gates, kv_cmp, kv_rest, bsz, t_len)
    h = _out_proj(o, bf(nsa_w_out), h, ln_g[1, 1], ln_b[1, 1], "nsa_out_proj")
    (h,) = _ffn(h, w_in_next, w_out_next, ln_g[1, 2], ln_b[1, 2])
    return h.reshape(bsz, t_len, d)
```

```python
import functools
import math

import numpy as np
import jax
import jax.numpy as jnp
from jax import lax
from jax.experimental import pallas as pl
from jax.experimental.pallas import tpu as pltpu

F32 = jnp.float32
BF16 = jnp.bfloat16

DEPTH = 2
DEEPNORM_ALPHA = (2 * DEPTH) ** 0.25
LN_EPS = 1e-5
MACARON_WEIGHT = 0.5

RET_HEADS = 8
RET_CHUNK = 256
RET_ROPE_BASE = 10000.0

NSA_HEADS = 16
NSA_KV_GROUPS = 4
NSA_HEADS_PER_GROUP = NSA_HEADS // NSA_KV_GROUPS
CMP_BLOCK = 32
CMP_STRIDE = 16
SLC_BLOCK = 64
SLC_TOP = 8
WINDOW = 512
FORCE_BONUS = 1e4
ROPE_THETA = 500000.0
NEG_INF = -1e30
LOG2_E = math.log2(math.e)

V7X_VMEM_BYTES = 64 * 1024 * 1024
VMEM_LIMIT = V7X_VMEM_BYTES - 8 * 1024 * 1024
LANES = 128

FFN_ROW_TILE = 512
FFN_HIDDEN_TILE = 512
FFN_SUBTILES = 2
PROJ_ROW_TILE = 1024
PROJ_COL_TILE = 1024
OUT_PROJ_ROW_TILE = 512
NSA_Q_TILE = 256
NSA_KEY_CHUNK = 512


def _params(*semantics):
    return pltpu.CompilerParams(dimension_semantics=semantics, vmem_limit_bytes=VMEM_LIMIT)


def _layer_norm(y, g, b, eps=LN_EPS):
    mu = jnp.mean(y, axis=-1, keepdims=True)
    d = y - mu
    var = jnp.mean(d * d, axis=-1, keepdims=True)
    return d * lax.rsqrt(var + eps) * g + b


def _dot(a, b):
    return jnp.dot(a, b, preferred_element_type=F32)


def _dot_nt(a, b):
    return lax.dot_general(a, b, (((1,), (1,)), ((), ())), preferred_element_type=F32)


def _dot_tn(a, b):
    return lax.dot_general(a, b, (((0,), (0,)), ((), ())), preferred_element_type=F32)


def _ffn_kernel(*refs, nf, cast_next, emit_bf16):
    refs = list(refs)
    x_ref, wa_ref, wu_ref, wo_ref, g_ref, b_ref = refs[:6]
    refs = refs[6:]
    if cast_next:
        nwi_ref, nwo_ref = refs[:2]
        refs = refs[2:]
    o_ref = refs.pop(0)
    ob_ref = refs.pop(0) if emit_bf16 else None
    xb_ref = refs.pop()
    if cast_next:
        nwi_out, nwo_out = refs
        nwi_out[...] = nwi_ref[...].astype(BF16)
        nwo_out[...] = nwo_ref[...].astype(BF16)
    f = pl.program_id(1)

    @pl.when(f == 0)
    def _():
        x = x_ref[...]
        xb_ref[...] = x.astype(BF16)
        o_ref[...] = (DEEPNORM_ALPHA / MACARON_WEIGHT) * x

    xb = xb_ref[...]
    sub = wa_ref.shape[1] // FFN_SUBTILES
    part = None
    for s in range(FFN_SUBTILES):
        a = _dot(xb, wa_ref[:, s * sub:(s + 1) * sub])
        u = _dot(xb, wu_ref[:, s * sub:(s + 1) * sub])
        h = (a * jax.nn.sigmoid(a)) * u
        p = _dot(h.astype(BF16), wo_ref[s * sub:(s + 1) * sub, :])
        part = p if part is None else part + p
    o_ref[...] += part

    @pl.when(f == nf - 1)
    def _():
        out = _layer_norm(o_ref[...], g_ref[...], b_ref[...], eps=LN_EPS / MACARON_WEIGHT ** 2)
        o_ref[...] = out
        if emit_bf16:
            ob_ref[...] = out.astype(BF16)


def _ffn(x, w_in, w_out, g, b, cast_next=None, emit_bf16=False):
    m, d = x.shape
    f_dim = w_out.shape[0]
    tm, tf = FFN_ROW_TILE, FFN_HIDDEN_TILE
    ni, nf = m // tm, f_dim // tf
    row_spec = pl.BlockSpec((tm, d), lambda i, f: (i, 0))
    vec_spec = pl.BlockSpec((1, d), lambda i, f: (0, 0))
    in_specs = [row_spec, pl.BlockSpec((d, tf), lambda i, f: (0, f)), pl.BlockSpec((d, tf), lambda i, f: (0, f + nf)),
                pl.BlockSpec((tf, d), lambda i, f: (f, 0)), vec_spec, vec_spec]
    args = [x, w_in, w_in, w_out, g.reshape(1, d), b.reshape(1, d)]
    out_specs, out_shape = [row_spec], [jax.ShapeDtypeStruct((m, d), F32)]
    if emit_bf16:
        out_specs.append(row_spec)
        out_shape.append(jax.ShapeDtypeStruct((m, d), BF16))
    if cast_next is not None:
        nw_in, nw_out, layer = cast_next
        ri, ci = d // ni, 2 * f_dim // nf
        ro = f_dim // (ni * nf)
        assert ri % 16 == 0 and ci % LANES == 0 and ro % 16 == 0
        in_specs += [pl.BlockSpec((None, ri, ci), lambda i, f: (layer, i, f)),
                     pl.BlockSpec((None, ro, d), lambda i, f: (layer, i * nf + f, 0))]
        out_specs += [pl.BlockSpec((ri, ci), lambda i, f: (i, f)), pl.BlockSpec((ro, d), lambda i, f: (i * nf + f, 0))]
        out_shape += [jax.ShapeDtypeStruct((d, 2 * f_dim), BF16), jax.ShapeDtypeStruct((f_dim, d), BF16)]
        args += [nw_in, nw_out]
    return pl.pallas_call(
        functools.partial(_ffn_kernel, nf=nf, cast_next=cast_next is not None, emit_bf16=emit_bf16),
        grid=(ni, nf),
        in_specs=in_specs,
        out_specs=out_specs,
        out_shape=out_shape,
        scratch_shapes=[pltpu.VMEM((tm, d), BF16)],
        compiler_params=_params("arbitrary", "arbitrary"),
        name="ffn_deepnorm",
    )(*args)


def _wproj_kernel(*refs, n_extra, epilogue):
    xb_ref, w_ref = refs[:2]
    extra = refs[2:2 + n_extra]
    o_ref, wb_ref = refs[2 + n_extra:]

    @pl.when(pl.program_id(1) == 0)
    def _():
        wb_ref[...] = w_ref[...].astype(BF16)

    epilogue(_dot(xb_ref[...], wb_ref[...]), pl.program_id(0), extra, o_ref)


def _wproj(xb, w, *, col0, ncols, epilogue, out_spec, out_shape, extra=(), extra_specs=(), name):
    m, k = xb.shape
    tm, tn = PROJ_ROW_TILE, min(PROJ_COL_TILE, ncols)
    assert col0 % tn == 0 and ncols % tn == 0
    return pl.pallas_call(
        functools.partial(_wproj_kernel, n_extra=len(extra), epilogue=epilogue),
        grid=(ncols // tn, m // tm),
        in_specs=[pl.BlockSpec((tm, k), lambda j, i: (i, 0)),
                  pl.BlockSpec((None, k, tn), lambda j, i: (0, 0, j + col0 // tn)), *extra_specs],
        out_specs=out_spec(tm, tn),
        out_shape=out_shape,
        scratch_shapes=[pltpu.VMEM((k, tn), BF16)],
        compiler_params=_params("arbitrary", "arbitrary"),
        name=name,
    )(xb, w, *extra)


def _table_specs(t_len, width, tm=PROJ_ROW_TILE):
    tpb = t_len // tm
    return [pl.BlockSpec((tm, width), lambda j, i: (i % tpb, 0))] * 2


def _qk_rot_epilogue(acc, j, extra, o_ref, *, q_tiles, k_scale, half):
    cos_ref, sin_ref = extra
    scale = jnp.where(j >= q_tiles, k_scale, 1.0).astype(F32)
    cos = cos_ref[...]
    sin = sin_ref[...]
    for h in range(acc.shape[1] // (2 * half)):
        c0 = h * 2 * half
        x1 = acc[:, c0:c0 + half]
        x2 = acc[:, c0 + half:c0 + 2 * half]
        o_ref[:, c0:c0 + half] = ((x1 * cos - x2 * sin) * scale).astype(o_ref.dtype)
        o_ref[:, c0 + half:c0 + 2 * half] = ((x2 * cos + x1 * sin) * scale).astype(o_ref.dtype)


def _store_epilogue(acc, j, extra, o_ref):
    o_ref[...] = acc.astype(o_ref.dtype)


def _partial_rotary(x, c, s, half):
    lane = lax.broadcasted_iota(jnp.int32, x.shape, 1)
    partner = jnp.where(lane < half, pltpu.roll(x, LANES - half, 1), pltpu.roll(x, half, 1))
    return x * c + partner * s


def _heads_epilogue(acc, j, extra, o_ref, *, hd, hps, half, scale, rotate, split):
    c_ref, s_ref = extra
    for h in range(acc.shape[1] // hd):
        piece = acc[:, h * hd:(h + 1) * hd]
        if rotate == "all" or (rotate == "even" and h < hps):
            piece = _partial_rotary(piece, c_ref[...], s_ref[...], half)
        if scale != 1.0:
            piece = piece * scale
        if split:
            o_ref[h // hps, h % hps] = piece.astype(o_ref.dtype)
        else:
            o_ref[:, h * hd:(h + 1) * hd] = piece.astype(o_ref.dtype)


def _sigmoid_epilogue(acc, j, extra, o_ref):
    o_ref[...] = jax.nn.sigmoid(acc).astype(o_ref.dtype)


def _head_proj(xb, w, c_tab, s_tab, t_len, *, col0, nslots, out_dtype, half, scale, rotate, split, name):
    m = xb.shape[0]
    hd, hps = LANES, NSA_KV_GROUPS
    assert nslots % 2 == 0
    if split:
        out_spec = lambda tm, tn: pl.BlockSpec((2, hps, tm, hd), lambda j, i: (j, 0, i, 0))
        out_shape = jax.ShapeDtypeStruct((nslots, hps, m, hd), out_dtype)
    else:
        out_spec = lambda tm, tn: pl.BlockSpec((tm, tn), lambda j, i: (i, j))
        out_shape = jax.ShapeDtypeStruct((m, nslots * hps * hd), out_dtype)
    epi = functools.partial(_heads_epilogue, hd=hd, hps=hps, half=half, scale=scale, rotate=rotate, split=split)
    return _wproj(xb, w, col0=col0, ncols=nslots * hps * hd, epilogue=epi, out_spec=out_spec, out_shape=out_shape,
                  extra=(c_tab, s_tab), extra_specs=_table_specs(t_len, hd), name=name)


def _out_proj_kernel(y_ref, w_ref, x_ref, g_ref, b_ref, o_ref, *, nsub):
    rs = y_ref.shape[0] // nsub
    for s in range(nsub):
        r = slice(s * rs, (s + 1) * rs)
        acc = _dot(y_ref[r, :], w_ref[...])
        o_ref[r, :] = _layer_norm(DEEPNORM_ALPHA * x_ref[r, :] + acc, g_ref[...], b_ref[...])


def _out_proj(y, w, x, g, b, name):
    m, k = y.shape
    d = w.shape[2]
    tm = OUT_PROJ_ROW_TILE
    row_spec = pl.BlockSpec((tm, d), lambda i: (i, 0))
    return pl.pallas_call(
        functools.partial(_out_proj_kernel, nsub=2),
        grid=(m // tm,),
        in_specs=[
            pl.BlockSpec((tm, k), lambda i: (i, 0)),
            pl.BlockSpec((None, k, d), lambda i: (0, 0, 0), pipeline_mode=pl.Buffered(1)),
            row_spec,
            pl.BlockSpec((1, d), lambda i: (0, 0)),
            pl.BlockSpec((1, d), lambda i: (0, 0)),
        ],
        out_specs=row_spec,
        out_shape=jax.ShapeDtypeStruct((m, d), F32),
        compiler_params=_params("parallel"),
        name=name,
    )(y, w, x, g.reshape(1, d), b.reshape(1, d))


def _retention_kernel(gam_ref, q_ref, k_ref, v_ref, g_ref, dec_ref, xi_ref, zeta_ref, gng_ref, gnb_ref,
                      o_ref, state_ref, *, nh, dk, dv):
    @pl.when(pl.program_id(1) == 0)
    def _():
        state_ref[...] = jnp.zeros_like(state_ref)

    for h in range(nh):
        q = q_ref[:, h * dk:(h + 1) * dk]
        k = k_ref[:, h * dk:(h + 1) * dk]
        v = v_ref[:, h * dv:(h + 1) * dv]
        state = state_ref[h]
        s = _dot_nt(q, k) * dec_ref[h]
        out = _dot(s.astype(BF16), v) + _dot(q, state.astype(BF16)) * xi_ref[h]
        kz = (k.astype(F32) * zeta_ref[h]).astype(BF16)
        state_ref[h] = _dot_tn(kz, v) + gam_ref[h] * state

        mu = jnp.mean(out, axis=-1, keepdims=True)
        d = out - mu
        var = jnp.mean(d * d, axis=-1, keepdims=True)
        cols = slice(h * dv, (h + 1) * dv)
        gn = d * lax.rsqrt(var + LN_EPS) * gng_ref[:, cols] + gnb_ref[:, cols]
        gate = g_ref[:, cols].astype(F32)
        o_ref[:, cols] = ((gate * jax.nn.sigmoid(gate)) * gn).astype(o_ref.dtype)


def _retention(qk, vg, gn_g, gn_b, bsz, t_len, dk, dv):
    h = RET_HEADS
    c = RET_CHUNK
    nc = t_len // c
    m = bsz * t_len
    log_gamma = jnp.log(1.0 - jnp.power(2.0, -5.0 - jnp.arange(h, dtype=F32)))
    idx = jnp.arange(c, dtype=F32)
    diff = idx[:, None] - idx[None, :]
    decay = jnp.where(diff >= 0, jnp.exp(log_gamma[:, None, None] * jnp.maximum(diff, 0.0)), 0.0)
    xi = jnp.exp(log_gamma[:, None] * (idx + 1.0))[..., None]
    zeta = jnp.exp(log_gamma[:, None] * (c - 1.0 - idx))[..., None]
    gamma_c = jnp.exp(log_gamma * c)
    row = lambda b, cc: b * nc + cc
    whole = lambda shape: pl.BlockSpec(shape, lambda b, cc: (0,) * len(shape))
    return pl.pallas_call(
        functools.partial(_retention_kernel, nh=h, dk=dk, dv=dv),
        grid=(bsz, nc),
        in_specs=[
            pl.BlockSpec(memory_space=pltpu.SMEM),
            pl.BlockSpec((c, h * dk), lambda b, cc: (row(b, cc), 0)),
            pl.BlockSpec((c, h * dk), lambda b, cc: (row(b, cc), 1)),
            pl.BlockSpec((c, h * dv), lambda b, cc: (row(b, cc), 0)),
            pl.BlockSpec((c, h * dv), lambda b, cc: (row(b, cc), 1)),
            whole((h, c, c)), whole((h, c, 1)), whole((h, c, 1)),
            whole((1, h * dv)), whole((1, h * dv)),
        ],
        out_specs=pl.BlockSpec((c, h * dv), lambda b, cc: (row(b, cc), 0)),
        out_shape=jax.ShapeDtypeStruct((m, h * dv), BF16),
        scratch_shapes=[pltpu.VMEM((h, dk, dv), F32)],
        compiler_params=_params("parallel", "arbitrary"),
        name="retention_chunks",
    )(gamma_c, qk, qk, vg, vg, decay, xi, zeta, gn_g.reshape(1, h * dv), gn_b.reshape(1, h * dv))


def _cmp_mlp_kernel(x_ref, pos_ref, w1a_ref, w1b_ref, b1_ref, w2_ref, c_ref, s_ref, o_ref, *, half):
    nrows = o_ref.shape[0]
    x = jnp.concatenate([x_ref[pl.ds(l, nrows, stride=CMP_STRIDE), :] for l in range(CMP_STRIDE)], axis=1)
    pos = pos_ref[...]
    a = _dot((x + pos[0:1, :]).astype(BF16), w1a_ref[...])
    bm = _dot((x + pos[1:2, :]).astype(BF16), w1b_ref[...])
    hid = jax.nn.gelu(a + pltpu.roll(bm, nrows - 1, 0) + b1_ref[...])
    comp = _dot(hid.astype(BF16), w2_ref[...])
    is_key = pl.program_id(0) == 0
    c = jnp.where(is_key, c_ref[...], 1.0)
    s = jnp.where(is_key, s_ref[...], 0.0)
    o_ref[...] = _partial_rotary(comp, c, s, half).astype(o_ref.dtype)


def _cmp_mlp(raw, cmp_pos, w1, b1, w2, c_tab, s_tab, bsz, t_len, half):
    g, hd = NSA_KV_GROUPS, LANES
    nrows = t_len // CMP_STRIDE
    feat = CMP_STRIDE * hd
    hidden = w1.shape[-1]
    pos = cmp_pos.reshape(2, CMP_BLOCK // CMP_STRIDE, feat)
    return pl.pallas_call(
        functools.partial(_cmp_mlp_kernel, half=half),
        grid=(2, g, bsz),
        in_specs=[
            pl.BlockSpec((None, None, t_len, hd), lambda c, gg, b: (c, gg, b, 0)),
            pl.BlockSpec((None, 2, feat), lambda c, gg, b: (c, 0, 0)),
            pl.BlockSpec((None, feat, hidden), lambda c, gg, b: (c, 0, 0)),
            pl.BlockSpec((None, feat, hidden), lambda c, gg, b: (c, 1, 0)),
            pl.BlockSpec((None, 1, hidden), lambda c, gg, b: (c, 0, 0)),
            pl.BlockSpec((None, hidden, hd), lambda c, gg, b: (c, 0, 0)),
            pl.BlockSpec((nrows, hd), lambda c, gg, b: (0, 0)),
            pl.BlockSpec((nrows, hd), lambda c, gg, b: (0, 0)),
        ],
        out_specs=pl.BlockSpec((None, None, None, nrows, hd), lambda c, gg, b: (c, gg, b, 0, 0)),
        out_shape=jax.ShapeDtypeStruct((2, g, bsz, nrows, hd), BF16),
        compiler_params=_params("arbitrary", "arbitrary", "arbitrary"),
        name="cmp_mlp",
    )(raw, pos, w1, w1, b1.reshape(2, 1, hidden), w2, c_tab, s_tab)


def _with_ones(v):
    return jnp.concatenate([v, jnp.ones_like(v)], axis=1)


def _nsa_kernel(q_ref, gate_ref, kc_ref, vc_ref, ks_ref, vs_ref, kw_ref, vw_ref, ovt_ref, ex_ref,
                o_ref, mask_ref, *, tq, tc, hg, hd, max_chunks):
    qi = pl.program_id(2)
    t0 = qi * tq
    rows = hg * tq
    q = q_ref[...]
    q4 = jnp.concatenate([q[:, i * hd:(i + 1) * hd] for i in range(hg)], axis=0)
    tpos = t0 + lax.broadcasted_iota(jnp.int32, (tq, 1), 0)

    ncmp = kc_ref.shape[0]
    s3 = _dot_nt(q4, kc_ref[...]).reshape(hg, tq, ncmp)
    cmp_end = lax.broadcasted_iota(jnp.int32, (tq, ncmp), 1) * CMP_STRIDE + (CMP_BLOCK - 1)
    s3 = jnp.where((cmp_end <= tpos)[None], s3, NEG_INF)
    e3 = jnp.exp2(s3 - jnp.max(s3, axis=-1, keepdims=True))
    has_cmp = (tpos >= CMP_BLOCK - 1).astype(F32)
    p3 = (e3 / jnp.sum(e3, axis=-1, keepdims=True) * has_cmp[None]).astype(BF16)
    o_cmp = _dot(p3.reshape(rows, ncmp), vc_ref[...])

    nslc = ovt_ref.shape[0]
    p_slc = _dot_nt(ovt_ref[...], jnp.concatenate([p3[i] for i in range(hg)], axis=1))
    trow = t0 + lax.broadcasted_iota(jnp.int32, (1, tq), 1)
    blk = lax.broadcasted_iota(jnp.int32, (nslc, tq), 0)
    cur = jnp.right_shift(trow, SLC_BLOCK.bit_length() - 1)
    valid = blk * SLC_BLOCK <= trow
    forced = (blk == 0) | (blk == cur) | (blk == cur - 1)
    score = jnp.where(valid, p_slc + jnp.where(forced, FORCE_BONUS, 0.0), -1.0)
    rank = jnp.zeros((nslc, tq), jnp.int32)
    for kb in range(nslc):
        ck = score[kb:kb + 1, :]
        rank += ((ck > score) | ((ck == score) & (blk > kb))).astype(jnp.int32)
    sel = jnp.where(rank < SLC_TOP, 1.0, 0.0).astype(BF16)
    mask_ref[...] = _dot_tn(sel, ex_ref[...])

    gts = gate_ref[...]
    lane = lax.broadcasted_iota(jnp.int32, (1, gts.shape[1]), 1)
    gate0 = pl.program_id(1) * (3 * hg)
    gate_col = [jnp.sum(jnp.where(lane == gate0 + c, gts, 0.0), axis=1, keepdims=True) for c in range(3 * hg)]

    def window_and_selected(n_chunks):
        wlen = WINDOW + tq
        w0 = pl.multiple_of(jnp.maximum(qi - WINDOW // tq, 0) * tq, tq)
        sw = _dot_nt(q4, kw_ref[pl.ds(w0, wlen), :]).reshape(hg, tq, wlen)
        kpos = w0 + lax.broadcasted_iota(jnp.int32, (tq, wlen), 1)
        okw = (kpos <= tpos) & (kpos > tpos - WINDOW)
        sw = jnp.where(okw[None], sw, NEG_INF)
        ew = jnp.exp2(sw - jnp.max(sw, axis=-1, keepdims=True)).astype(BF16)
        rw = _dot(ew.reshape(rows, wlen), _with_ones(vw_ref[pl.ds(w0, wlen), :]))
        o_win = rw[:, :hd] / rw[:, hd:]

        m_i = acc = None
        for c in range(n_chunks):
            k0 = c * tc
            s = _dot_nt(q4, ks_ref[k0:k0 + tc, :]).reshape(hg, tq, tc)
            ok = mask_ref[:, k0:k0 + tc] > 0.5
            if c == n_chunks - 1:
                ok &= (k0 + lax.broadcasted_iota(jnp.int32, (tq, tc), 1)) <= tpos
            s = jnp.where(ok[None], s, NEG_INF)
            m_c = jnp.max(s, axis=-1, keepdims=True)
            m_new = m_c if c == 0 else jnp.maximum(m_i, m_c)
            pv = _dot(jnp.exp2(s - m_new).astype(BF16).reshape(rows, tc), _with_ones(vs_ref[k0:k0 + tc, :]))
            acc = pv if c == 0 else acc * jnp.exp2(m_i - m_new).reshape(rows, 1) + pv
            m_i = m_new
        o_slc = acc[:, :hd] / acc[:, hd:]

        for i in range(hg):
            r = slice(i * tq, (i + 1) * tq)
            o = gate_col[3 * i] * o_cmp[r] + gate_col[3 * i + 1] * o_slc[r] + gate_col[3 * i + 2] * o_win[r]
            o_ref[:, i * hd:(i + 1) * hd] = o.astype(o_ref.dtype)

    n_chunks = lax.div(t0, tc) + 1
    for n in range(1, max_chunks + 1):
        pl.when(n_chunks == n)(functools.partial(window_and_selected, n))


def _nsa_attention(q, gates, kv_cmp, kv_rest, bsz, t_len):
    g, hg, hd = NSA_KV_GROUPS, NSA_HEADS_PER_GROUP, LANES
    tq, tc = NSA_Q_TILE, NSA_KEY_CHUNK
    nq = t_len // tq
    ncmp_pad = t_len // CMP_STRIDE
    n_cmp = (t_len - CMP_BLOCK) // CMP_STRIDE + 1
    n_slc = t_len // SLC_BLOCK
    cs = np.arange(ncmp_pad)[None, :] * CMP_STRIDE
    ss = np.arange(n_slc)[:, None] * SLC_BLOCK
    ovt = ((cs < ss + SLC_BLOCK) & (cs + CMP_BLOCK > ss) & (np.arange(ncmp_pad)[None, :] < n_cmp)).astype(np.float32)
    ovt = jnp.asarray(np.tile(ovt, (1, hg)), BF16)
    expand = jnp.asarray((np.arange(t_len)[None, :] // SLC_BLOCK == np.arange(n_slc)[:, None]).astype(np.float32), BF16)
    kv_spec = lambda slot: pl.BlockSpec((None, None, t_len, hd), lambda b, gg, qi: (slot, gg, b, 0))
    cmp_spec = lambda slot: pl.BlockSpec((None, None, None, ncmp_pad, hd), lambda b, gg, qi: (slot, gg, b, 0, 0))
    return pl.pallas_call(
        functools.partial(_nsa_kernel, tq=tq, tc=tc, hg=hg, hd=hd, max_chunks=t_len // tc),
        grid=(bsz, g, nq),
        in_specs=[
            pl.BlockSpec((tq, hg * hd), lambda b, gg, qi: (b * nq + qi, gg)),
            pl.BlockSpec((tq, LANES), lambda b, gg, qi: (b * nq + qi, 0)),
            cmp_spec(0), cmp_spec(1),
            kv_spec(0), kv_spec(1), kv_spec(2), kv_spec(3),
            pl.BlockSpec((n_slc, hg * ncmp_pad), lambda b, gg, qi: (0, 0)),
            pl.BlockSpec((n_slc, t_len), lambda b, gg, qi: (0, 0)),
        ],
        out_specs=pl.BlockSpec((tq, hg * hd), lambda b, gg, qi: (b * nq + qi, gg)),
        out_shape=jax.ShapeDtypeStruct((bsz * t_len, g * hg * hd), BF16),
        scratch_shapes=[pltpu.VMEM((tq, t_len), F32)],
        compiler_params=_params("parallel", "parallel", "arbitrary"),
        name="nsa_attention",
    )(q, gates, kv_cmp, kv_cmp, kv_rest, kv_rest, kv_rest, kv_rest, ovt, expand)


def _rope_tables(pos, rot_dims, theta):
    half = rot_dims // 2
    inv_freq = jnp.power(jnp.float32(theta), -jnp.arange(half, dtype=F32) / half)
    ang = pos.astype(F32)[:, None] * inv_freq[None, :]
    cos, sin = jnp.cos(ang), jnp.sin(ang)
    n = pos.shape[0]
    pad = LANES - rot_dims
    c_tab = jnp.concatenate([cos, cos, jnp.ones((n, pad), F32)], axis=1)
    s_tab = jnp.concatenate([-sin, sin, jnp.zeros((n, pad), F32)], axis=1)
    return c_tab, s_tab


def kernel(x, ln_g, ln_b, ffn1_w_in, ffn1_w_out, ffn2_w_in, ffn2_w_out, ret_w_in, ret_gn_g, ret_gn_b, ret_w_out, kv_w, cmp_pos, cmp_w1, cmp_b1, cmp_w2, nsa_w_q, nsa_w_out):
    bsz, t_len, d = x.shape
    m = bsz * t_len
    bf = lambda w: w.astype(BF16)
    h = x.reshape(m, d)
    pos = jnp.arange(t_len)

    h, hb, w_in_next, w_out_next = _ffn(h, bf(ffn1_w_in[0]), bf(ffn1_w_out[0]), ln_g[0, 0], ln_b[0, 0],
                                        cast_next=(ffn2_w_in, ffn2_w_out, 0), emit_bf16=True)
    dk = d // RET_HEADS
    dv = 2 * d // RET_HEADS
    half = dk // 2
    inv_freq = jnp.power(jnp.float32(RET_ROPE_BASE), -jnp.arange(half, dtype=F32) / half)
    ang = pos.astype(F32)[:, None] * inv_freq[None, :]
    n_qk, n_vg = 2 * RET_HEADS * dk, 2 * RET_HEADS * dv
    tile_spec = lambda tm, tn: pl.BlockSpec((tm, tn), lambda j, i: (i, j))
    rot_epi = functools.partial(_qk_rot_epilogue, q_tiles=RET_HEADS * dk // PROJ_COL_TILE, k_scale=dk ** -0.5,
                                half=half)
    qk = _wproj(hb, ret_w_in, col0=0, ncols=n_qk, epilogue=rot_epi, out_spec=tile_spec,
                out_shape=jax.ShapeDtypeStruct((m, n_qk), BF16),
                extra=(jnp.cos(ang), jnp.sin(ang)), extra_specs=_table_specs(t_len, half), name="ret_qk_proj")
    vg = _wproj(hb, ret_w_in, col0=n_qk, ncols=n_vg, epilogue=_store_epilogue, out_spec=tile_spec,
                out_shape=jax.ShapeDtypeStruct((m, n_vg), BF16), name="ret_vg_proj")
    y = _retention(qk, vg, ret_gn_g[0], ret_gn_b[0], bsz, t_len, dk, dv)
    h = _out_proj(y, bf(ret_w_out), h, ln_g[0, 1], ln_b[0, 1], "ret_out_proj")
    h, hb, w_in_next, w_out_next = _ffn(h, w_in_next, w_out_next, ln_g[0, 2], ln_b[0, 2],
                                        cast_next=(ffn1_w_in, ffn1_w_out, 1), emit_bf16=True)

    hd = d // NSA_HEADS
    rope_dims = hd // 4
    rope_half = rope_dims // 2
    c_tab, s_tab = _rope_tables(pos, rope_dims, ROPE_THETA)
    w_kv = kv_w[None]
    slot_cols = NSA_KV_GROUPS * hd
    raw_cmp = _head_proj(hb, w_kv, c_tab, s_tab, t_len, col0=0, nslots=2, out_dtype=F32, half=rope_half,
                         scale=1.0, rotate="none", split=True, name="kv_cmp_proj")
    kv_rest = _head_proj(hb, w_kv, c_tab, s_tab, t_len, col0=2 * slot_cols, nslots=4, out_dtype=BF16,
                         half=rope_half, scale=1.0, rotate="even", split=True, name="kv_slc_win_proj")
    n_rows = t_len // CMP_STRIDE
    cmp_end = jnp.arange(n_rows) * CMP_STRIDE + CMP_BLOCK - 1
    cc_tab, cs_tab = _rope_tables(cmp_end, rope_dims, ROPE_THETA)
    kv_cmp = _cmp_mlp(raw_cmp, cmp_pos, bf(cmp_w1), cmp_b1, bf(cmp_w2), cc_tab, cs_tab, bsz, t_len, rope_half)

    h, hb, w_in_next, w_out_next = _ffn(h, w_in_next, w_out_next, ln_g[1, 0], ln_b[1, 0],
                                        cast_next=(ffn2_w_in, ffn2_w_out, 1), emit_bf16=True)
    n_q = NSA_HEADS * hd
    q = _head_proj(hb, nsa_w_q, c_tab, s_tab, t_len, col0=0, nslots=n_q // slot_cols, out_dtype=BF16,
                   half=rope_half, scale=hd ** -0.5 * LOG2_E, rotate="all", split=False, name="nsa_q_proj")
    n_gate = NSA_HEADS * 3
    w_gate = jnp.pad(nsa_w_q[0][:, n_q:], ((0, 0), (0, LANES - n_gate)))[None]
    gates = _wproj(hb, w_gate, col0=0, ncols=LANES, epilogue=_sigmoid_epilogue,
                   out_spec=lambda tm, tn: pl.BlockSpec((tm, tn), lambda j, i: (i, j)),
                   out_shape=jax.ShapeDtypeStruct((m, LANES), F32), name="nsa_gate_proj")
    o = _nsa_attention(q, gates, kv_cmp, kv_rest, bsz, t_len)
    h = _out_proj(o, bf(nsa_w_out), h, ln_g[1, 1], ln_b[1, 1], "nsa_out_proj")
    (h,) = _ffn(h, w_in_next, w_out_next, ln_g[1, 2], ln_b[1, 2])
    return h.reshape(bsz, t_len, d)
```

```python
import functools
import math

import numpy as np
import jax
import jax.numpy as jnp
from jax import lax
from jax.experimental import pallas as pl
from jax.experimental.pallas import tpu as pltpu

F32 = jnp.float32
BF16 = jnp.bfloat16

DEPTH = 2
DEEPNORM_ALPHA = (2 * DEPTH) ** 0.25
LN_EPS = 1e-5
MACARON_WEIGHT = 0.5

RET_HEADS = 8
RET_CHUNK = 256
RET_ROPE_BASE = 10000.0

NSA_HEADS = 16
NSA_KV_GROUPS = 4
NSA_HEADS_PER_GROUP = NSA_HEADS // NSA_KV_GROUPS
CMP_BLOCK = 32
CMP_STRIDE = 16
SLC_BLOCK = 64
SLC_TOP = 8
WINDOW = 512
FORCE_BONUS = 1e4
ROPE_THETA = 500000.0
NEG_INF = -1e30
LOG2_E = math.log2(math.e)

V7X_VMEM_BYTES = 64 * 1024 * 1024
VMEM_LIMIT = V7X_VMEM_BYTES - 8 * 1024 * 1024
LANES = 128

FFN_ROW_TILE = 512
FFN_HIDDEN_TILE = 512
FFN_SUBTILES = 2
PROJ_ROW_TILE = 1024
PROJ_COL_TILE = 1024
OUT_PROJ_ROW_TILE = 512
NSA_Q_TILE = 256
NSA_KEY_CHUNK = 512


def _params(*semantics):
    return pltpu.CompilerParams(dimension_semantics=semantics, vmem_limit_bytes=VMEM_LIMIT)


def _layer_norm(y, g, b, eps=LN_EPS):
    mu = jnp.mean(y, axis=-1, keepdims=True)
    d = y - mu
    var = jnp.mean(d * d, axis=-1, keepdims=True)
    return d * lax.rsqrt(var + eps) * g + b


def _dot(a, b):
    return jnp.dot(a, b, preferred_element_type=F32)


def _dot_nt(a, b):
    return lax.dot_general(a, b, (((1,), (1,)), ((), ())), preferred_element_type=F32)


def _dot_tn(a, b):
    return lax.dot_general(a, b, (((0,), (0,)), ((), ())), preferred_element_type=F32)


def _ffn_kernel(*refs, nf, cast_next, emit_bf16):
    refs = list(refs)
    x_ref, wa_ref, wu_ref, wo_ref, g_ref, b_ref = refs[:6]
    refs = refs[6:]
    if cast_next:
        nwi_ref, nwo_ref = refs[:2]
        refs = refs[2:]
    o_ref = refs.pop(0)
    ob_ref = refs.pop(0) if emit_bf16 else None
    xb_ref = refs.pop()
    if cast_next:
        nwi_out, nwo_out = refs
        nwi_out[...] = nwi_ref[...].astype(BF16)
        nwo_out[...] = nwo_ref[...].astype(BF16)
    f = pl.program_id(1)

    @pl.when(f == 0)
    def _():
        x = x_ref[...]
        xb_ref[...] = x.astype(BF16)
        o_ref[...] = (DEEPNORM_ALPHA / MACARON_WEIGHT) * x

    xb = xb_ref[...]
    sub = wa_ref.shape[1] // FFN_SUBTILES
    part = None
    for s in range(FFN_SUBTILES):
        a = _dot(xb, wa_ref[:, s * sub:(s + 1) * sub])
        u = _dot(xb, wu_ref[:, s * sub:(s + 1) * sub])
        h = (a * jax.nn.sigmoid(a)) * u
        p = _dot(h.astype(BF16), wo_ref[s * sub:(s + 1) * sub, :])
        part = p if part is None else part + p
    o_ref[...] += part

    @pl.when(f == nf - 1)
    def _():
        out = _layer_norm(o_ref[...], g_ref[...], b_ref[...], eps=LN_EPS / MACARON_WEIGHT ** 2)
        o_ref[...] = out
        if emit_bf16:
            ob_ref[...] = out.astype(BF16)


def _ffn(x, w_in, w_out, g, b, cast_next=None, emit_bf16=False):
    m, d = x.shape
    f_dim = w_out.shape[0]
    tm, tf = FFN_ROW_TILE, FFN_HIDDEN_TILE
    ni, nf = m // tm, f_dim // tf
    row_spec = pl.BlockSpec((tm, d), lambda i, f: (i, 0))
    vec_spec = pl.BlockSpec((1, d), lambda i, f: (0, 0))
    in_specs = [row_spec, pl.BlockSpec((d, tf), lambda i, f: (0, f)), pl.BlockSpec((d, tf), lambda i, f: (0, f + nf)),
                pl.BlockSpec((tf, d), lambda i, f: (f, 0)), vec_spec, vec_spec]
    args = [x, w_in, w_in, w_out, g.reshape(1, d), b.reshape(1, d)]
    out_specs, out_shape = [row_spec], [jax.ShapeDtypeStruct((m, d), F32)]
    if emit_bf16:
        out_specs.append(row_spec)
        out_shape.append(jax.ShapeDtypeStruct((m, d), BF16))
    if cast_next is not None:
        nw_in, nw_out, layer = cast_next
        ri, ci = d // ni, 2 * f_dim // nf
        ro = f_dim // (ni * nf)
        assert ri % 16 == 0 and ci % LANES == 0 and ro % 16 == 0
        in_specs += [pl.BlockSpec((None, ri, ci), lambda i, f: (layer, i, f)),
                     pl.BlockSpec((None, ro, d), lambda i, f: (layer, i * nf + f, 0))]
        out_specs += [pl.BlockSpec((ri, ci), lambda i, f: (i, f)), pl.BlockSpec((ro, d), lambda i, f: (i * nf + f, 0))]
        out_shape += [jax.ShapeDtypeStruct((d, 2 * f_dim), BF16), jax.ShapeDtypeStruct((f_dim, d), BF16)]
        args += [nw_in, nw_out]
    return pl.pallas_call(
        functools.partial(_ffn_kernel, nf=nf, cast_next=cast_next is not None, emit_bf16=emit_bf16),
        grid=(ni, nf),
        in_specs=in_specs,
        out_specs=out_specs,
        out_shape=out_shape,
        scratch_shapes=[pltpu.VMEM((tm, d), BF16)],
        compiler_params=_params("arbitrary", "arbitrary"),
        name="ffn_deepnorm",
    )(*args)


def _wproj_kernel(*refs, n_extra, epilogue):
    xb_ref, w_ref = refs[:2]
    extra = refs[2:2 + n_extra]
    o_ref, wb_ref = refs[2 + n_extra:]

    @pl.when(pl.program_id(1) == 0)
    def _():
        wb_ref[...] = w_ref[...].astype(BF16)

    epilogue(_dot(xb_ref[...], wb_ref[...]), pl.program_id(0), extra, o_ref)


def _wproj(xb, w, *, col0, ncols, epilogue, out_spec, out_shape, extra=(), extra_specs=(), name):
    m, k = xb.shape
    tm, tn = PROJ_ROW_TILE, min(PROJ_COL_TILE, ncols)
    assert col0 % tn == 0 and ncols % tn == 0
    return pl.pallas_call(
        functools.partial(_wproj_kernel, n_extra=len(extra), epilogue=epilogue),
        grid=(ncols // tn, m // tm),
        in_specs=[pl.BlockSpec((tm, k), lambda j, i: (i, 0)),
                  pl.BlockSpec((None, k, tn), lambda j, i: (0, 0, j + col0 // tn)), *extra_specs],
        out_specs=out_spec(tm, tn),
        out_shape=out_shape,
        scratch_shapes=[pltpu.VMEM((k, tn), BF16)],
        compiler_params=_params("arbitrary", "arbitrary"),
        name=name,
    )(xb, w, *extra)


def _table_specs(t_len, width, tm=PROJ_ROW_TILE):
    tpb = t_len // tm
    return [pl.BlockSpec((tm, width), lambda j, i: (i % tpb, 0))] * 2


def _qk_rot_epilogue(acc, j, extra, o_ref, *, q_tiles, k_scale, half):
    cos_ref, sin_ref = extra
    scale = jnp.where(j >= q_tiles, k_scale, 1.0).astype(F32)
    cos = cos_ref[...]
    sin = sin_ref[...]
    for h in range(acc.shape[1] // (2 * half)):
        c0 = h * 2 * half
        x1 = acc[:, c0:c0 + half]
        x2 = acc[:, c0 + half:c0 + 2 * half]
        o_ref[:, c0:c0 + half] = ((x1 * cos - x2 * sin) * scale).astype(o_ref.dtype)
        o_ref[:, c0 + half:c0 + 2 * half] = ((x2 * cos + x1 * sin) * scale).astype(o_ref.dtype)


def _store_epilogue(acc, j, extra, o_ref):
    o_ref[...] = acc.astype(o_ref.dtype)


def _partial_rotary(x, c, s, half):
    lane = lax.broadcasted_iota(jnp.int32, x.shape, 1)
    partner = jnp.where(lane < half, pltpu.roll(x, LANES - half, 1), pltpu.roll(x, half, 1))
    return x * c + partner * s


def _heads_epilogue(acc, j, extra, o_ref, *, hd, hps, half, scale, rotate, split):
    c_ref, s_ref = extra
    for h in range(acc.shape[1] // hd):
        piece = acc[:, h * hd:(h + 1) * hd]
        if rotate == "all" or (rotate == "even" and h < hps):
            piece = _partial_rotary(piece, c_ref[...], s_ref[...], half)
        if scale != 1.0:
            piece = piece * scale
        if split:
            o_ref[h // hps, h % hps] = piece.astype(o_ref.dtype)
        else:
            o_ref[:, h * hd:(h + 1) * hd] = piece.astype(o_ref.dtype)


def _sigmoid_epilogue(acc, j, extra, o_ref):
    o_ref[...] = jax.nn.sigmoid(acc).astype(o_ref.dtype)


def _head_proj(xb, w, c_tab, s_tab, t_len, *, col0, nslots, out_dtype, half, scale, rotate, split, name):
    m = xb.shape[0]
    hd, hps = LANES, NSA_KV_GROUPS
    assert nslots % 2 == 0
    if split:
        out_spec = lambda tm, tn: pl.BlockSpec((2, hps, tm, hd), lambda j, i: (j, 0, i, 0))
        out_shape = jax.ShapeDtypeStruct((nslots, hps, m, hd), out_dtype)
    else:
        out_spec = lambda tm, tn: pl.BlockSpec((tm, tn), lambda j, i: (i, j))
        out_shape = jax.ShapeDtypeStruct((m, nslots * hps * hd), out_dtype)
    epi = functools.partial(_heads_epilogue, hd=hd, hps=hps, half=half, scale=scale, rotate=rotate, split=split)
    return _wproj(xb, w, col0=col0, ncols=nslots * hps * hd, epilogue=epi, out_spec=out_spec, out_shape=out_shape,
                  extra=(c_tab, s_tab), extra_specs=_table_specs(t_len, hd), name=name)


def _out_proj_kernel(y_ref, w_ref, x_ref, g_ref, b_ref, o_ref, *, nsub):
    rs = y_ref.shape[0] // nsub
    for s in range(nsub):
        r = slice(s * rs, (s + 1) * rs)
        acc = _dot(y_ref[r, :], w_ref[...])
        o_ref[r, :] = _layer_norm(DEEPNORM_ALPHA * x_ref[r, :] + acc, g_ref[...], b_ref[...])


def _out_proj(y, w, x, g, b, name):
    m, k = y.shape
    d = w.shape[2]
    tm = OUT_PROJ_ROW_TILE
    row_spec = pl.BlockSpec((tm, d), lambda i: (i, 0))
    return pl.pallas_call(
        functools.partial(_out_proj_kernel, nsub=2),
        grid=(m // tm,),
        in_specs=[
            pl.BlockSpec((tm, k), lambda i: (i, 0)),
            pl.BlockSpec((None, k, d), lambda i: (0, 0, 0), pipeline_mode=pl.Buffered(1)),
            row_spec,
            pl.BlockSpec((1, d), lambda i: (0, 0)),
            pl.BlockSpec((1, d), lambda i: (0, 0)),
        ],
        out_specs=row_spec,
        out_shape=jax.ShapeDtypeStruct((m, d), F32),
        compiler_params=_params("parallel"),
        name=name,
    )(y, w, x, g.reshape(1, d), b.reshape(1, d))


def _retention_kernel(gam_ref, q_ref, k_ref, v_ref, g_ref, dec_ref, xi_ref, zeta_ref, gng_ref, gnb_ref,
                      o_ref, state_ref, *, nh, dk, dv):
    @pl.when(pl.program_id(1) == 0)
    def _():
        state_ref[...] = jnp.zeros_like(state_ref)

    for h in range(nh):
        q = q_ref[:, h * dk:(h + 1) * dk]
        k = k_ref[:, h * dk:(h + 1) * dk]
        v = v_ref[:, h * dv:(h + 1) * dv]
        state = state_ref[h]
        s = _dot_nt(q, k) * dec_ref[h]
        out = _dot(s.astype(BF16), v) + _dot(q, state.astype(BF16)) * xi_ref[h]
        kz = (k.astype(F32) * zeta_ref[h]).astype(BF16)
        state_ref[h] = _dot_tn(kz, v) + gam_ref[h] * state

        mu = jnp.mean(out, axis=-1, keepdims=True)
        d = out - mu
        var = jnp.mean(d * d, axis=-1, keepdims=True)
        cols = slice(h * dv, (h + 1) * dv)
        gn = d * lax.rsqrt(var + LN_EPS) * gng_ref[:, cols] + gnb_ref[:, cols]
        gate = g_ref[:, cols].astype(F32)
        o_ref[:, cols] = ((gate * jax.nn.sigmoid(gate)) * gn).astype(o_ref.dtype)


def _retention(qk, vg, gn_g, gn_b, bsz, t_len, dk, dv):
    h = RET_HEADS
    c = RET_CHUNK
    nc = t_len // c
    m = bsz * t_len
    log_gamma = jnp.log(1.0 - jnp.power(2.0, -5.0 - jnp.arange(h, dtype=F32)))
    idx = jnp.arange(c, dtype=F32)
    diff = idx[:, None] - idx[None, :]
    decay = jnp.where(diff >= 0, jnp.exp(log_gamma[:, None, None] * jnp.maximum(diff, 0.0)), 0.0)
    xi = jnp.exp(log_gamma[:, None] * (idx + 1.0))[..., None]
    zeta = jnp.exp(log_gamma[:, None] * (c - 1.0 - idx))[..., None]
    gamma_c = jnp.exp(log_gamma * c)
    row = lambda b, cc: b * nc + cc
    whole = lambda shape: pl.BlockSpec(shape, lambda b, cc: (0,) * len(shape))
    return pl.pallas_call(
        functools.partial(_retention_kernel, nh=h, dk=dk, dv=dv),
        grid=(bsz, nc),
        in_specs=[
            pl.BlockSpec(memory_space=pltpu.SMEM),
            pl.BlockSpec((c, h * dk), lambda b, cc: (row(b, cc), 0)),
            pl.BlockSpec((c, h * dk), lambda b, cc: (row(b, cc), 1)),
            pl.BlockSpec((c, h * dv), lambda b, cc: (row(b, cc), 0)),
            pl.BlockSpec((c, h * dv), lambda b, cc: (row(b, cc), 1)),
            whole((h, c, c)), whole((h, c, 1)), whole((h, c, 1)),
            whole((1, h * dv)), whole((1, h * dv)),
        ],
        out_specs=pl.BlockSpec((c, h * dv), lambda b, cc: (row(b, cc), 0)),
        out_shape=jax.ShapeDtypeStruct((m, h * dv), BF16),
        scratch_shapes=[pltpu.VMEM((h, dk, dv), F32)],
        compiler_params=_params("parallel", "arbitrary"),
        name="retention_chunks",
    )(gamma_c, qk, qk, vg, vg, decay, xi, zeta, gn_g.reshape(1, h * dv), gn_b.reshape(1, h * dv))


def _cmp_mlp_kernel(x_ref, pos_ref, w1a_ref, w1b_ref, b1_ref, w2_ref, c_ref, s_ref, o_ref, *, half):
    nrows = o_ref.shape[0]
    x = jnp.concatenate([x_ref[pl.ds(l, nrows, stride=CMP_STRIDE), :] for l in range(CMP_STRIDE)], axis=1)
    pos = pos_ref[...]
    a = _dot((x + pos[0:1, :]).astype(BF16), w1a_ref[...])
    bm = _dot((x + pos[1:2, :]).astype(BF16), w1b_ref[...])
    hid = jax.nn.gelu(a + pltpu.roll(bm, nrows - 1, 0) + b1_ref[...])
    comp = _dot(hid.astype(BF16), w2_ref[...])
    is_key = pl.program_id(0) == 0
    c = jnp.where(is_key, c_ref[...], 1.0)
    s = jnp.where(is_key, s_ref[...], 0.0)
    o_ref[...] = _partial_rotary(comp, c, s, half).astype(o_ref.dtype)


def _cmp_mlp(raw, cmp_pos, w1, b1, w2, c_tab, s_tab, bsz, t_len, half):
    g, hd = NSA_KV_GROUPS, LANES
    nrows = t_len // CMP_STRIDE
    feat = CMP_STRIDE * hd
    hidden = w1.shape[-1]
    pos = cmp_pos.reshape(2, CMP_BLOCK // CMP_STRIDE, feat)
    return pl.pallas_call(
        functools.partial(_cmp_mlp_kernel, half=half),
        grid=(2, g, bsz),
        in_specs=[
            pl.BlockSpec((None, None, t_len, hd), lambda c, gg, b: (c, gg, b, 0)),
            pl.BlockSpec((None, 2, feat), lambda c, gg, b: (c, 0, 0)),
            pl.BlockSpec((None, feat, hidden), lambda c, gg, b: (c, 0, 0)),
            pl.BlockSpec((None, feat, hidden), lambda c, gg, b: (c, 1, 0)),
            pl.BlockSpec((None, 1, hidden), lambda c, gg, b: (c, 0, 0)),
            pl.BlockSpec((None, hidden, hd), lambda c, gg, b: (c, 0, 0)),
            pl.BlockSpec((nrows, hd), lambda c, gg, b: (0, 0)),
            pl.BlockSpec((nrows, hd), lambda c, gg, b: (0, 0)),
        ],
        out_specs=pl.BlockSpec((None, None, None, nrows, hd), lambda c, gg, b: (c, gg, b, 0, 0)),
        out_shape=jax.ShapeDtypeStruct((2, g, bsz, nrows, hd), BF16),
        compiler_params=_params("arbitrary", "arbitrary", "arbitrary"),
        name="cmp_mlp",
    )(raw, pos, w1, w1, b1.reshape(2, 1, hidden), w2, c_tab, s_tab)


def _with_ones(v):
    return jnp.concatenate([v, jnp.ones_like(v)], axis=1)


def _nsa_kernel(q_ref, gate_ref, kc_ref, vc_ref, ks_ref, vs_ref, kw_ref, vw_ref, ovt_ref, ex_ref,
                o_ref, mask_ref, *, tq, tc, hg, hd, max_chunks):
    qi = pl.program_id(2)
    t0 = qi * tq
    rows = hg * tq
    q = q_ref[...]
    q4 = jnp.concatenate([q[:, i * hd:(i + 1) * hd] for i in range(hg)], axis=0)
    tpos = t0 + lax.broadcasted_iota(jnp.int32, (tq, 1), 0)

    ncmp = kc_ref.shape[0]
    s3 = _dot_nt(q4, kc_ref[...]).reshape(hg, tq, ncmp)
    cmp_end = lax.broadcasted_iota(jnp.int32, (tq, ncmp), 1) * CMP_STRIDE + (CMP_BLOCK - 1)
    s3 = jnp.where((cmp_end <= tpos)[None], s3, NEG_INF)
    e3 = jnp.exp2(s3 - jnp.max(s3, axis=-1, keepdims=True))
    has_cmp = (tpos >= CMP_BLOCK - 1).astype(F32)
    p3 = (e3 / jnp.sum(e3, axis=-1, keepdims=True) * has_cmp[None]).astype(BF16)
    o_cmp = _dot(p3.reshape(rows, ncmp), vc_ref[...])

    nslc = ovt_ref.shape[0]
    p_slc = _dot_nt(ovt_ref[...], jnp.concatenate([p3[i] for i in range(hg)], axis=1))
    trow = t0 + lax.broadcasted_iota(jnp.int32, (1, tq), 1)
    blk = lax.broadcasted_iota(jnp.int32, (nslc, tq), 0)
    cur = jnp.right_shift(trow, SLC_BLOCK.bit_length() - 1)
    valid = blk * SLC_BLOCK <= trow
    forced = (blk == 0) | (blk == cur) | (blk == cur - 1)
    score = jnp.where(valid, p_slc + jnp.where(forced, FORCE_BONUS, 0.0), -1.0)
    rank = jnp.zeros((nslc, tq), jnp.int32)
    for kb in range(nslc):
        ck = score[kb:kb + 1, :]
        rank += ((ck > score) | ((ck == score) & (blk > kb))).astype(jnp.int32)
    sel = jnp.where(rank < SLC_TOP, 1.0, 0.0).astype(BF16)
    mask_ref[...] = _dot_tn(sel, ex_ref[...])

    gts = gate_ref[...]
    lane = lax.broadcasted_iota(jnp.int32, (1, gts.shape[1]), 1)
    gate0 = pl.program_id(1) * (3 * hg)
    gate_col = [jnp.sum(jnp.where(lane == gate0 + c, gts, 0.0), axis=1, keepdims=True) for c in range(3 * hg)]

    def window_and_selected(n_chunks, last_len):
        wlen = WINDOW + tq
        w0 = pl.multiple_of(jnp.maximum(qi - WINDOW // tq, 0) * tq, tq)
        sw = _dot_nt(q4, kw_ref[pl.ds(w0, wlen), :]).reshape(hg, tq, wlen)
        kpos = w0 + lax.broadcasted_iota(jnp.int32, (tq, wlen), 1)
        okw = (kpos <= tpos) & (kpos > tpos - WINDOW)
        sw = jnp.where(okw[None], sw, NEG_INF)
        ew = jnp.exp2(sw - jnp.max(sw, axis=-1, keepdims=True)).astype(BF16)
        rw = _dot(ew.reshape(rows, wlen), _with_ones(vw_ref[pl.ds(w0, wlen), :]))
        o_win = rw[:, :hd] / rw[:, hd:]

        m_i = acc = None
        for c in range(n_chunks):
            k0 = c * tc
            kl = last_len if c == n_chunks - 1 else tc
            s = _dot_nt(q4, ks_ref[k0:k0 + kl, :]).reshape(hg, tq, kl)
            ok = mask_ref[:, k0:k0 + kl] > 0.5
            if c == n_chunks - 1:
                ok &= (k0 + lax.broadcasted_iota(jnp.int32, (tq, kl), 1)) <= tpos
            s = jnp.where(ok[None], s, NEG_INF)
            m_c = jnp.max(s, axis=-1, keepdims=True)
            m_new = m_c if c == 0 else jnp.maximum(m_i, m_c)
            pv = _dot(jnp.exp2(s - m_new).astype(BF16).reshape(rows, kl), _with_ones(vs_ref[k0:k0 + kl, :]))
            acc = pv if c == 0 else acc * jnp.exp2(m_i - m_new).reshape(rows, 1) + pv
            m_i = m_new
        o_slc = acc[:, :hd] / acc[:, hd:]

        for i in range(hg):
            r = slice(i * tq, (i + 1) * tq)
            o = gate_col[3 * i] * o_cmp[r] + gate_col[3 * i + 1] * o_slc[r] + gate_col[3 * i + 2] * o_win[r]
            o_ref[:, i * hd:(i + 1) * hd] = o.astype(o_ref.dtype)

    n_chunks = lax.div(t0, tc) + 1
    offset = lax.rem(t0, tc)
    for n in range(1, max_chunks + 1):
        for off in range(0, tc, tq):
            pl.when((n_chunks == n) & (offset == off))(functools.partial(window_and_selected, n, off + tq))


def _nsa_attention(q, gates, kv_cmp, kv_rest, bsz, t_len):
    g, hg, hd = NSA_KV_GROUPS, NSA_HEADS_PER_GROUP, LANES
    tq, tc = NSA_Q_TILE, NSA_KEY_CHUNK
    nq = t_len // tq
    ncmp_pad = t_len // CMP_STRIDE
    n_cmp = (t_len - CMP_BLOCK) // CMP_STRIDE + 1
    n_slc = t_len // SLC_BLOCK
    cs = np.arange(ncmp_pad)[None, :] * CMP_STRIDE
    ss = np.arange(n_slc)[:, None] * SLC_BLOCK
    ovt = ((cs < ss + SLC_BLOCK) & (cs + CMP_BLOCK > ss) & (np.arange(ncmp_pad)[None, :] < n_cmp)).astype(np.float32)
    ovt = jnp.asarray(np.tile(ovt, (1, hg)), BF16)
    expand = jnp.asarray((np.arange(t_len)[None, :] // SLC_BLOCK == np.arange(n_slc)[:, None]).astype(np.float32), BF16)
    kv_spec = lambda slot: pl.BlockSpec((None, None, t_len, hd), lambda b, gg, qi: (slot, gg, b, 0))
    cmp_spec = lambda slot: pl.BlockSpec((None, None, None, ncmp_pad, hd), lambda b, gg, qi: (slot, gg, b, 0, 0))
    return pl.pallas_call(
        functools.partial(_nsa_kernel, tq=tq, tc=tc, hg=hg, hd=hd, max_chunks=t_len // tc),
        grid=(bsz, g, nq),
        in_specs=[
            pl.BlockSpec((tq, hg * hd), lambda b, gg, qi: (b * nq + qi, gg)),
            pl.BlockSpec((tq, LANES), lambda b, gg, qi: (b * nq + qi, 0)),
            cmp_spec(0), cmp_spec(1),
            kv_spec(0), kv_spec(1), kv_spec(2), kv_spec(3),
            pl.BlockSpec((n_slc, hg * ncmp_pad), lambda b, gg, qi: (0, 0)),
            pl.BlockSpec((n_slc, t_len), lambda b, gg, qi: (0, 0)),
        ],
        out_specs=pl.BlockSpec((tq, hg * hd), lambda b, gg, qi: (b * nq + qi, gg)),
        out_shape=jax.ShapeDtypeStruct((bsz * t_len, g * hg * hd), BF16),
        scratch_shapes=[pltpu.VMEM((tq, t_len), F32)],
        compiler_params=_params("parallel", "parallel", "arbitrary"),
        name="nsa_attention",
    )(q, gates, kv_cmp, kv_cmp, kv_rest, kv_rest, kv_rest, kv_rest, ovt, expand)


def _rope_tables(pos, rot_dims, theta):
    half = rot_dims // 2
    inv_freq = jnp.power(jnp.float32(theta), -jnp.arange(half, dtype=F32) / half)
    ang = pos.astype(F32)[:, None] * inv_freq[None, :]
    cos, sin = jnp.cos(ang), jnp.sin(ang)
    n = pos.shape[0]
    pad = LANES - rot_dims
    c_tab = jnp.concatenate([cos, cos, jnp.ones((n, pad), F32)], axis=1)
    s_tab = jnp.concatenate([-sin, sin, jnp.zeros((n, pad), F32)], axis=1)
    return c_tab, s_tab


def kernel(x, ln_g, ln_b, ffn1_w_in, ffn1_w_out, ffn2_w_in, ffn2_w_out, ret_w_in, ret_gn_g, ret_gn_b, ret_w_out, kv_w, cmp_pos, cmp_w1, cmp_b1, cmp_w2, nsa_w_q, nsa_w_out):
    bsz, t_len, d = x.shape
    m = bsz * t_len
    bf = lambda w: w.astype(BF16)
    h = x.reshape(m, d)
    pos = jnp.arange(t_len)

    h, hb, w_in_next, w_out_next = _ffn(h, bf(ffn1_w_in[0]), bf(ffn1_w_out[0]), ln_g[0, 0], ln_b[0, 0],
                                        cast_next=(ffn2_w_in, ffn2_w_out, 0), emit_bf16=True)
    dk = d // RET_HEADS
    dv = 2 * d // RET_HEADS
    half = dk // 2
    inv_freq = jnp.power(jnp.float32(RET_ROPE_BASE), -jnp.arange(half, dtype=F32) / half)
    ang = pos.astype(F32)[:, None] * inv_freq[None, :]
    n_qk, n_vg = 2 * RET_HEADS * dk, 2 * RET_HEADS * dv
    tile_spec = lambda tm, tn: pl.BlockSpec((tm, tn), lambda j, i: (i, j))
    rot_epi = functools.partial(_qk_rot_epilogue, q_tiles=RET_HEADS * dk // PROJ_COL_TILE, k_scale=dk ** -0.5,
                                half=half)
    qk = _wproj(hb, ret_w_in, col0=0, ncols=n_qk, epilogue=rot_epi, out_spec=tile_spec,
                out_shape=jax.ShapeDtypeStruct((m, n_qk), BF16),
                extra=(jnp.cos(ang), jnp.sin(ang)), extra_specs=_table_specs(t_len, half), name="ret_qk_proj")
    vg = _wproj(hb, ret_w_in, col0=n_qk, ncols=n_vg, epilogue=_store_epilogue, out_spec=tile_spec,
                out_shape=jax.ShapeDtypeStruct((m, n_vg), BF16), name="ret_vg_proj")
    y = _retention(qk, vg, ret_gn_g[0], ret_gn_b[0], bsz, t_len, dk, dv)
    h = _out_proj(y, bf(ret_w_out), h, ln_g[0, 1], ln_b[0, 1], "ret_out_proj")
    h, hb, w_in_next, w_out_next = _ffn(h, w_in_next, w_out_next, ln_g[0, 2], ln_b[0, 2],
                                        cast_next=(ffn1_w_in, ffn1_w_out, 1), emit_bf16=True)

    hd = d // NSA_HEADS
    rope_dims = hd // 4
    rope_half = rope_dims // 2
    c_tab, s_tab = _rope_tables(pos, rope_dims, ROPE_THETA)
    w_kv = kv_w[None]
    slot_cols = NSA_KV_GROUPS * hd
    raw_cmp = _head_proj(hb, w_kv, c_tab, s_tab, t_len, col0=0, nslots=2, out_dtype=F32, half=rope_half,
                         scale=1.0, rotate="none", split=True, name="kv_cmp_proj")
    kv_rest = _head_proj(hb, w_kv, c_tab, s_tab, t_len, col0=2 * slot_cols, nslots=4, out_dtype=BF16,
                         half=rope_half, scale=1.0, rotate="even", split=True, name="kv_slc_win_proj")
    n_rows = t_len // CMP_STRIDE
    cmp_end = jnp.arange(n_rows) * CMP_STRIDE + CMP_BLOCK - 1
    cc_tab, cs_tab = _rope_tables(cmp_end, rope_dims, ROPE_THETA)
    kv_cmp = _cmp_mlp(raw_cmp, cmp_pos, bf(cmp_w1), cmp_b1, bf(cmp_w2), cc_tab, cs_tab, bsz, t_len, rope_half)

    h, hb, w_in_next, w_out_next = _ffn(h, w_in_next, w_out_next, ln_g[1, 0], ln_b[1, 0],
                                        cast_next=(ffn2_w_in, ffn2_w_out, 1), emit_bf16=True)
    n_q = NSA_HEADS * hd
    q = _head_proj(hb, nsa_w_q, c_tab, s_tab, t_len, col0=0, nslots=n_q // slot_cols, out_dtype=BF16,
                   half=rope_half, scale=hd ** -0.5 * LOG2_E, rotate="all", split=False, name="nsa_q_proj")
    n_gate = NSA_HEADS * 3
    w_gate = jnp.pad(nsa_w_q[0][:, n_q:], ((0, 0), (0, LANES - n_gate)))[None]
    gates = _wproj(hb, w_gate, col0=0, ncols=LANES, epilogue=_sigmoid_epilogue,
                   out_spec=lambda tm, tn: pl.BlockSpec((tm, tn), lambda j, i: (i, j)),
                   out_shape=jax.ShapeDtypeStruct((m, LANES), F32), name="nsa_gate_proj")
    o = _nsa_attention(q, gates, kv_cmp, kv_rest, bsz, t_len)
    h = _out_proj(o, bf(nsa_w_out), h, ln_g[1, 1], ln_b[1, 1], "nsa_out_proj")
    (h,) = _ffn(h, w_in_next, w_out_next, ln_g[1, 2], ln_b[1, 2])
    return h.reshape(bsz, t_len, d)
```

```python
import functools
import math

import numpy as np
import jax
import jax.numpy as jnp
from jax import lax
from jax.experimental import pallas as pl
from jax.experimental.pallas import tpu as pltpu

F32 = jnp.float32
BF16 = jnp.bfloat16

DEPTH = 2
DEEPNORM_ALPHA = (2 * DEPTH) ** 0.25
LN_EPS = 1e-5
MACARON_WEIGHT = 0.5

RET_HEADS = 8
RET_CHUNK = 256
RET_ROPE_BASE = 10000.0

NSA_HEADS = 16
NSA_KV_GROUPS = 4
NSA_HEADS_PER_GROUP = NSA_HEADS // NSA_KV_GROUPS
CMP_BLOCK = 32
CMP_STRIDE = 16
SLC_BLOCK = 64
SLC_TOP = 8
WINDOW = 512
FORCE_BONUS = 1e4
ROPE_THETA = 500000.0
NEG_INF = -1e30
LOG2_E = math.log2(math.e)

V7X_VMEM_BYTES = 64 * 1024 * 1024
VMEM_LIMIT = V7X_VMEM_BYTES - 8 * 1024 * 1024
LANES = 128

FFN_ROW_TILE = 512
FFN_HIDDEN_TILE = 512
FFN_SUBTILES = 2
PROJ_ROW_TILE = 1024
PROJ_COL_TILE = 1024
OUT_PROJ_ROW_TILE = 512
NSA_Q_TILE = 256
NSA_KEY_CHUNK = 512


def _params(*semantics):
    return pltpu.CompilerParams(dimension_semantics=semantics, vmem_limit_bytes=VMEM_LIMIT)


def _layer_norm(y, g, b, eps=LN_EPS):
    mu = jnp.mean(y, axis=-1, keepdims=True)
    d = y - mu
    var = jnp.mean(d * d, axis=-1, keepdims=True)
    return d * lax.rsqrt(var + eps) * g + b


def _dot(a, b):
    return jnp.dot(a, b, preferred_element_type=F32)


def _dot_nt(a, b):
    return lax.dot_general(a, b, (((1,), (1,)), ((), ())), preferred_element_type=F32)


def _dot_tn(a, b):
    return lax.dot_general(a, b, (((0,), (0,)), ((), ())), preferred_element_type=F32)


def _ffn_kernel(*refs, nf, cast_next, emit_bf16):
    refs = list(refs)
    x_ref, wa_ref, wu_ref, wo_ref, g_ref, b_ref = refs[:6]
    refs = refs[6:]
    if cast_next:
        nwi_ref, nwo_ref = refs[:2]
        refs = refs[2:]
    o_ref = refs.pop(0)
    ob_ref = refs.pop(0) if emit_bf16 else None
    xb_ref = refs.pop()
    if cast_next:
        nwi_out, nwo_out = refs
        nwi_out[...] = nwi_ref[...].astype(BF16)
        nwo_out[...] = nwo_ref[...].astype(BF16)
    f = pl.program_id(1)

    @pl.when(f == 0)
    def _():
        x = x_ref[...]
        xb_ref[...] = x.astype(BF16)
        o_ref[...] = (DEEPNORM_ALPHA / MACARON_WEIGHT) * x

    xb = xb_ref[...]
    sub = wa_ref.shape[1] // FFN_SUBTILES
    part = None
    for s in range(FFN_SUBTILES):
        a = _dot(xb, wa_ref[:, s * sub:(s + 1) * sub])
        u = _dot(xb, wu_ref[:, s * sub:(s + 1) * sub])
        h = (a * jax.nn.sigmoid(a)) * u
        p = _dot(h.astype(BF16), wo_ref[s * sub:(s + 1) * sub, :])
        part = p if part is None else part + p
    o_ref[...] += part

    @pl.when(f == nf - 1)
    def _():
        out = _layer_norm(o_ref[...], g_ref[...], b_ref[...], eps=LN_EPS / MACARON_WEIGHT ** 2)
        o_ref[...] = out
        if emit_bf16:
            ob_ref[...] = out.astype(BF16)


def _ffn(x, w_in, w_out, g, b, cast_next=None, emit_bf16=False):
    m, d = x.shape
    f_dim = w_out.shape[0]
    tm, tf = FFN_ROW_TILE, FFN_HIDDEN_TILE
    ni, nf = m // tm, f_dim // tf
    row_spec = pl.BlockSpec((tm, d), lambda i, f: (i, 0))
    vec_spec = pl.BlockSpec((1, d), lambda i, f: (0, 0))
    in_specs = [row_spec, pl.BlockSpec((d, tf), lambda i, f: (0, f)), pl.BlockSpec((d, tf), lambda i, f: (0, f + nf)),
                pl.BlockSpec((tf, d), lambda i, f: (f, 0)), vec_spec, vec_spec]
    args = [x, w_in, w_in, w_out, g.reshape(1, d), b.reshape(1, d)]
    out_specs, out_shape = [row_spec], [jax.ShapeDtypeStruct((m, d), F32)]
    if emit_bf16:
        out_specs.append(row_spec)
        out_shape.append(jax.ShapeDtypeStruct((m, d), BF16))
    if cast_next is not None:
        nw_in, nw_out, layer = cast_next
        ri, ci = d // ni, 2 * f_dim // nf
        ro = f_dim // (ni * nf)
        assert ri % 16 == 0 and ci % LANES == 0 and ro % 16 == 0
        in_specs += [pl.BlockSpec((None, ri, ci), lambda i, f: (layer, i, f)),
                     pl.BlockSpec((None, ro, d), lambda i, f: (layer, i * nf + f, 0))]
        out_specs += [pl.BlockSpec((ri, ci), lambda i, f: (i, f)), pl.BlockSpec((ro, d), lambda i, f: (i * nf + f, 0))]
        out_shape += [jax.ShapeDtypeStruct((d, 2 * f_dim), BF16), jax.ShapeDtypeStruct((f_dim, d), BF16)]
        args += [nw_in, nw_out]
    return pl.pallas_call(
        functools.partial(_ffn_kernel, nf=nf, cast_next=cast_next is not None, emit_bf16=emit_bf16),
        grid=(ni, nf),
        in_specs=in_specs,
        out_specs=out_specs,
        out_shape=out_shape,
        scratch_shapes=[pltpu.VMEM((tm, d), BF16)],
        compiler_params=_params("arbitrary", "arbitrary"),
        name="ffn_deepnorm",
    )(*args)


def _wproj_kernel(*refs, n_extra, epilogue):
    xb_ref, w_ref = refs[:2]
    extra = refs[2:2 + n_extra]
    o_ref, wb_ref = refs[2 + n_extra:]

    @pl.when(pl.program_id(1) == 0)
    def _():
        wb_ref[...] = w_ref[...].astype(BF16)

    epilogue(_dot(xb_ref[...], wb_ref[...]), pl.program_id(0), extra, o_ref)


def _wproj(xb, w, *, col0, ncols, epilogue, out_spec, out_shape, extra=(), extra_specs=(), name):
    m, k = xb.shape
    tm, tn = PROJ_ROW_TILE, min(PROJ_COL_TILE, ncols)
    assert col0 % tn == 0 and ncols % tn == 0
    return pl.pallas_call(
        functools.partial(_wproj_kernel, n_extra=len(extra), epilogue=epilogue),
        grid=(ncols // tn, m // tm),
        in_specs=[pl.BlockSpec((tm, k), lambda j, i: (i, 0)),
                  pl.BlockSpec((None, k, tn), lambda j, i: (0, 0, j + col0 // tn)), *extra_specs],
        out_specs=out_spec(tm, tn),
        out_shape=out_shape,
        scratch_shapes=[pltpu.VMEM((k, tn), BF16)],
        compiler_params=_params("arbitrary", "arbitrary"),
        name=name,
    )(xb, w, *extra)


def _table_specs(t_len, width, tm=PROJ_ROW_TILE):
    tpb = t_len // tm
    return [pl.BlockSpec((tm, width), lambda j, i: (i % tpb, 0))] * 2


def _qk_rot_epilogue(acc, j, extra, o_ref, *, q_tiles, k_scale, half):
    cos_ref, sin_ref = extra
    scale = jnp.where(j >= q_tiles, k_scale, 1.0).astype(F32)
    cos = cos_ref[...]
    sin = sin_ref[...]
    for h in range(acc.shape[1] // (2 * half)):
        c0 = h * 2 * half
        x1 = acc[:, c0:c0 + half]
        x2 = acc[:, c0 + half:c0 + 2 * half]
        o_ref[:, c0:c0 + half] = ((x1 * cos - x2 * sin) * scale).astype(o_ref.dtype)
        o_ref[:, c0 + half:c0 + 2 * half] = ((x2 * cos + x1 * sin) * scale).astype(o_ref.dtype)


def _store_epilogue(acc, j, extra, o_ref):
    o_ref[...] = acc.astype(o_ref.dtype)


def _partial_rotary(x, c, s, half):
    lane = lax.broadcasted_iota(jnp.int32, x.shape, 1)
    partner = jnp.where(lane < half, pltpu.roll(x, LANES - half, 1), pltpu.roll(x, half, 1))
    return x * c + partner * s


def _heads_epilogue(acc, j, extra, o_ref, *, hd, hps, half, scale, rotate, split):
    c_ref, s_ref = extra
    for h in range(acc.shape[1] // hd):
        piece = acc[:, h * hd:(h + 1) * hd]
        if rotate == "all" or (rotate == "even" and h < hps):
            piece = _partial_rotary(piece, c_ref[...], s_ref[...], half)
        if scale != 1.0:
            piece = piece * scale
        if split:
            o_ref[h // hps, h % hps] = piece.astype(o_ref.dtype)
        else:
            o_ref[:, h * hd:(h + 1) * hd] = piece.astype(o_ref.dtype)


def _sigmoid_epilogue(acc, j, extra, o_ref):
    o_ref[...] = jax.nn.sigmoid(acc).astype(o_ref.dtype)


def _head_proj(xb, w, c_tab, s_tab, t_len, *, col0, nslots, out_dtype, half, scale, rotate, split, name):
    m = xb.shape[0]
    hd, hps = LANES, NSA_KV_GROUPS
    assert nslots % 2 == 0
    if split:
        out_spec = lambda tm, tn: pl.BlockSpec((2, hps, tm, hd), lambda j, i: (j, 0, i, 0))
        out_shape = jax.ShapeDtypeStruct((nslots, hps, m, hd), out_dtype)
    else:
        out_spec = lambda tm, tn: pl.BlockSpec((tm, tn), lambda j, i: (i, j))
        out_shape = jax.ShapeDtypeStruct((m, nslots * hps * hd), out_dtype)
    epi = functools.partial(_heads_epilogue, hd=hd, hps=hps, half=half, scale=scale, rotate=rotate, split=split)
    return _wproj(xb, w, col0=col0, ncols=nslots * hps * hd, epilogue=epi, out_spec=out_spec, out_shape=out_shape,
                  extra=(c_tab, s_tab), extra_specs=_table_specs(t_len, hd), name=name)


def _out_proj_kernel(y_ref, w_ref, x_ref, g_ref, b_ref, o_ref, *, nsub):
    rs = y_ref.shape[0] // nsub
    for s in range(nsub):
        r = slice(s * rs, (s + 1) * rs)
        acc = _dot(y_ref[r, :], w_ref[...])
        o_ref[r, :] = _layer_norm(DEEPNORM_ALPHA * x_ref[r, :] + acc, g_ref[...], b_ref[...])


def _out_proj(y, w, x, g, b, name):
    m, k = y.shape
    d = w.shape[2]
    tm = OUT_PROJ_ROW_TILE
    row_spec = pl.BlockSpec((tm, d), lambda i: (i, 0))
    return pl.pallas_call(
        functools.partial(_out_proj_kernel, nsub=2),
        grid=(m // tm,),
        in_specs=[
            pl.BlockSpec((tm, k), lambda i: (i, 0)),
            pl.BlockSpec((None, k, d), lambda i: (0, 0, 0), pipeline_mode=pl.Buffered(1)),
            row_spec,
            pl.BlockSpec((1, d), lambda i: (0, 0)),
            pl.BlockSpec((1, d), lambda i: (0, 0)),
        ],
        out_specs=row_spec,
        out_shape=jax.ShapeDtypeStruct((m, d), F32),
        compiler_params=_params("parallel"),
        name=name,
    )(y, w, x, g.reshape(1, d), b.reshape(1, d))


def _retention_kernel(gam_ref, q_ref, k_ref, v_ref, g_ref, dec_ref, xi_ref, zeta_ref, gng_ref, gnb_ref,
                      o_ref, state_ref, *, nh, dk, dv):
    @pl.when(pl.program_id(1) == 0)
    def _():
        state_ref[...] = jnp.zeros_like(state_ref)

    for h in range(nh):
        q = q_ref[:, h * dk:(h + 1) * dk]
        k = k_ref[:, h * dk:(h + 1) * dk]
        v = v_ref[:, h * dv:(h + 1) * dv]
        state = state_ref[h]
        s = _dot_nt(q, k) * dec_ref[h]
        out = _dot(s.astype(BF16), v) + _dot(q, state.astype(BF16)) * xi_ref[h]
        kz = (k.astype(F32) * zeta_ref[h]).astype(BF16)
        state_ref[h] = _dot_tn(kz, v) + gam_ref[h] * state

        mu = jnp.mean(out, axis=-1, keepdims=True)
        d = out - mu
        var = jnp.mean(d * d, axis=-1, keepdims=True)
        cols = slice(h * dv, (h + 1) * dv)
        gn = d * lax.rsqrt(var + LN_EPS) * gng_ref[:, cols] + gnb_ref[:, cols]
        gate = g_ref[:, cols].astype(F32)
        o_ref[:, cols] = ((gate * jax.nn.sigmoid(gate)) * gn).astype(o_ref.dtype)


def _retention(qk, vg, gn_g, gn_b, bsz, t_len, dk, dv):
    h = RET_HEADS
    c = RET_CHUNK
    nc = t_len // c
    m = bsz * t_len
    log_gamma = jnp.log(1.0 - jnp.power(2.0, -5.0 - jnp.arange(h, dtype=F32)))
    idx = jnp.arange(c, dtype=F32)
    diff = idx[:, None] - idx[None, :]
    decay = jnp.where(diff >= 0, jnp.exp(log_gamma[:, None, None] * jnp.maximum(diff, 0.0)), 0.0)
    xi = jnp.exp(log_gamma[:, None] * (idx + 1.0))[..., None]
    zeta = jnp.exp(log_gamma[:, None] * (c - 1.0 - idx))[..., None]
    gamma_c = jnp.exp(log_gamma * c)
    row = lambda b, cc: b * nc + cc
    whole = lambda shape: pl.BlockSpec(shape, lambda b, cc: (0,) * len(shape))
    return pl.pallas_call(
        functools.partial(_retention_kernel, nh=h, dk=dk, dv=dv),
        grid=(bsz, nc),
        in_specs=[
            pl.BlockSpec(memory_space=pltpu.SMEM),
            pl.BlockSpec((c, h * dk), lambda b, cc: (row(b, cc), 0)),
            pl.BlockSpec((c, h * dk), lambda b, cc: (row(b, cc), 1)),
            pl.BlockSpec((c, h * dv), lambda b, cc: (row(b, cc), 0)),
            pl.BlockSpec((c, h * dv), lambda b, cc: (row(b, cc), 1)),
            whole((h, c, c)), whole((h, c, 1)), whole((h, c, 1)),
            whole((1, h * dv)), whole((1, h * dv)),
        ],
        out_specs=pl.BlockSpec((c, h * dv), lambda b, cc: (row(b, cc), 0)),
        out_shape=jax.ShapeDtypeStruct((m, h * dv), BF16),
        scratch_shapes=[pltpu.VMEM((h, dk, dv), F32)],
        compiler_params=_params("parallel", "arbitrary"),
        name="retention_chunks",
    )(gamma_c, qk, qk, vg, vg, decay, xi, zeta, gn_g.reshape(1, h * dv), gn_b.reshape(1, h * dv))


def _cmp_mlp_kernel(x_ref, pos_ref, w1a_ref, w1b_ref, b1_ref, w2_ref, c_ref, s_ref, o_ref, *, half):
    nrows = o_ref.shape[0]
    x = jnp.concatenate([x_ref[pl.ds(l, nrows, stride=CMP_STRIDE), :] for l in range(CMP_STRIDE)], axis=1)
    pos = pos_ref[...]
    a = _dot((x + pos[0:1, :]).astype(BF16), w1a_ref[...])
    bm = _dot((x + pos[1:2, :]).astype(BF16), w1b_ref[...])
    hid = jax.nn.gelu(a + pltpu.roll(bm, nrows - 1, 0) + b1_ref[...])
    comp = _dot(hid.astype(BF16), w2_ref[...])
    is_key = pl.program_id(0) == 0
    c = jnp.where(is_key, c_ref[...], 1.0)
    s = jnp.where(is_key, s_ref[...], 0.0)
    o_ref[...] = _partial_rotary(comp, c, s, half).astype(o_ref.dtype)


def _cmp_mlp(raw, cmp_pos, w1, b1, w2, c_tab, s_tab, bsz, t_len, half):
    g, hd = NSA_KV_GROUPS, LANES
    nrows = t_len // CMP_STRIDE
    feat = CMP_STRIDE * hd
    hidden = w1.shape[-1]
    pos = cmp_pos.reshape(2, CMP_BLOCK // CMP_STRIDE, feat)
    return pl.pallas_call(
        functools.partial(_cmp_mlp_kernel, half=half),
        grid=(2, g, bsz),
        in_specs=[
            pl.BlockSpec((None, None, t_len, hd), lambda c, gg, b: (c, gg, b, 0)),
            pl.BlockSpec((None, 2, feat), lambda c, gg, b: (c, 0, 0)),
            pl.BlockSpec((None, feat, hidden), lambda c, gg, b: (c, 0, 0)),
            pl.BlockSpec((None, feat, hidden), lambda c, gg, b: (c, 1, 0)),
            pl.BlockSpec((None, 1, hidden), lambda c, gg, b: (c, 0, 0)),
            pl.BlockSpec((None, hidden, hd), lambda c, gg, b: (c, 0, 0)),
            pl.BlockSpec((nrows, hd), lambda c, gg, b: (0, 0)),
            pl.BlockSpec((nrows, hd), lambda c, gg, b: (0, 0)),
        ],
        out_specs=pl.BlockSpec((None, None, None, nrows, hd), lambda c, gg, b: (c, gg, b, 0, 0)),
        out_shape=jax.ShapeDtypeStruct((2, g, bsz, nrows, hd), BF16),
        compiler_params=_params("arbitrary", "arbitrary", "arbitrary"),
        name="cmp_mlp",
    )(raw, pos, w1, w1, b1.reshape(2, 1, hidden), w2, c_tab, s_tab)


def _with_ones(v):
    return jnp.concatenate([v, jnp.ones_like(v)], axis=1)


def _nsa_kernel(q_ref, gate_ref, kc_ref, vc_ref, ks_ref, vs_ref, kw_ref, vw_ref, ovt_ref, ex_ref,
                o_ref, mask_ref, *, tq, tc, hg, hd, max_chunks):
    qi = pl.program_id(2)
    t0 = qi * tq
    rows = hg * tq
    q = q_ref[...]
    q4 = jnp.concatenate([q[:, i * hd:(i + 1) * hd] for i in range(hg)], axis=0)
    tpos = t0 + lax.broadcasted_iota(jnp.int32, (tq, 1), 0)

    ncmp = kc_ref.shape[0]
    s3 = _dot_nt(q4, kc_ref[...]).reshape(hg, tq, ncmp)
    cmp_end = lax.broadcasted_iota(jnp.int32, (tq, ncmp), 1) * CMP_STRIDE + (CMP_BLOCK - 1)
    s3 = jnp.where((cmp_end <= tpos)[None], s3, NEG_INF)
    e3 = jnp.exp2(s3 - jnp.max(s3, axis=-1, keepdims=True))
    has_cmp = (tpos >= CMP_BLOCK - 1).astype(F32)
    p3 = (e3 / jnp.sum(e3, axis=-1, keepdims=True) * has_cmp[None]).astype(BF16)
    o_cmp = _dot(p3.reshape(rows, ncmp), vc_ref[...])

    nslc = ovt_ref.shape[0]
    p_slc = _dot_nt(ovt_ref[...], jnp.concatenate([p3[i] for i in range(hg)], axis=1))
    trow = t0 + lax.broadcasted_iota(jnp.int32, (1, tq), 1)
    blk = lax.broadcasted_iota(jnp.int32, (nslc, tq), 0)
    cur = jnp.right_shift(trow, SLC_BLOCK.bit_length() - 1)
    valid = blk * SLC_BLOCK <= trow
    forced = (blk == 0) | (blk == cur) | (blk == cur - 1)
    score = jnp.where(valid, p_slc + jnp.where(forced, FORCE_BONUS, 0.0), -1.0)
    rank = jnp.zeros((nslc, tq), jnp.int32)
    for kb in range(nslc):
        ck = score[kb:kb + 1, :]
        rank += ((ck > score) | ((ck == score) & (blk > kb))).astype(jnp.int32)
    sel = jnp.where(rank < SLC_TOP, 1.0, 0.0).astype(BF16)
    mask_ref[...] = _dot_tn(sel, ex_ref[...])

    gts = gate_ref[...]
    lane = lax.broadcasted_iota(jnp.int32, (1, gts.shape[1]), 1)
    gate0 = pl.program_id(1) * (3 * hg)
    gate_col = [jnp.sum(jnp.where(lane == gate0 + c, gts, 0.0), axis=1, keepdims=True) for c in range(3 * hg)]

    def window_and_selected(n_chunks, last_len):
        wlen = min(WINDOW + tq, (n_chunks - 1) * tc + last_len)
        w0 = pl.multiple_of(jnp.maximum(qi - WINDOW // tq, 0) * tq, tq)
        sw = _dot_nt(q4, kw_ref[pl.ds(w0, wlen), :]).reshape(hg, tq, wlen)
        kpos = w0 + lax.broadcasted_iota(jnp.int32, (tq, wlen), 1)
        okw = (kpos <= tpos) & (kpos > tpos - WINDOW)
        sw = jnp.where(okw[None], sw, NEG_INF)
        ew = jnp.exp2(sw - jnp.max(sw, axis=-1, keepdims=True)).astype(BF16)
        rw = _dot(ew.reshape(rows, wlen), _with_ones(vw_ref[pl.ds(w0, wlen), :]))
        o_win = rw[:, :hd] / rw[:, hd:]

        m_i = acc = None
        for c in range(n_chunks):
            k0 = c * tc
            kl = last_len if c == n_chunks - 1 else tc
            s = _dot_nt(q4, ks_ref[k0:k0 + kl, :]).reshape(hg, tq, kl)
            ok = mask_ref[:, k0:k0 + kl] > 0.5
            if c == n_chunks - 1:
                ok &= (k0 + lax.broadcasted_iota(jnp.int32, (tq, kl), 1)) <= tpos
            s = jnp.where(ok[None], s, NEG_INF)
            m_c = jnp.max(s, axis=-1, keepdims=True)
            m_new = m_c if c == 0 else jnp.maximum(m_i, m_c)
            pv = _dot(jnp.exp2(s - m_new).astype(BF16).reshape(rows, kl), _with_ones(vs_ref[k0:k0 + kl, :]))
            acc = pv if c == 0 else acc * jnp.exp2(m_i - m_new).reshape(rows, 1) + pv
            m_i = m_new
        o_slc = acc[:, :hd] / acc[:, hd:]

        for i in range(hg):
            r = slice(i * tq, (i + 1) * tq)
            o = gate_col[3 * i] * o_cmp[r] + gate_col[3 * i + 1] * o_slc[r] + gate_col[3 * i + 2] * o_win[r]
            o_ref[:, i * hd:(i + 1) * hd] = o.astype(o_ref.dtype)

    n_chunks = lax.div(t0, tc) + 1
    offset = lax.rem(t0, tc)
    for n in range(1, max_chunks + 1):
        for off in range(0, tc, tq):
            pl.when((n_chunks == n) & (offset == off))(functools.partial(window_and_selected, n, off + tq))


def _nsa_attention(q, gates, kv_cmp, kv_rest, bsz, t_len):
    g, hg, hd = NSA_KV_GROUPS, NSA_HEADS_PER_GROUP, LANES
    tq, tc = NSA_Q_TILE, NSA_KEY_CHUNK
    nq = t_len // tq
    ncmp_pad = t_len // CMP_STRIDE
    n_cmp = (t_len - CMP_BLOCK) // CMP_STRIDE + 1
    n_slc = t_len // SLC_BLOCK
    cs = np.arange(ncmp_pad)[None, :] * CMP_STRIDE
    ss = np.arange(n_slc)[:, None] * SLC_BLOCK
    ovt = ((cs < ss + SLC_BLOCK) & (cs + CMP_BLOCK > ss) & (np.arange(ncmp_pad)[None, :] < n_cmp)).astype(np.float32)
    ovt = jnp.asarray(np.tile(ovt, (1, hg)), BF16)
    expand = jnp.asarray((np.arange(t_len)[None, :] // SLC_BLOCK == np.arange(n_slc)[:, None]).astype(np.float32), BF16)
    kv_spec = lambda slot: pl.BlockSpec((None, None, t_len, hd), lambda b, gg, qi: (slot, gg, b, 0))
    cmp_spec = lambda slot: pl.BlockSpec((None, None, None, ncmp_pad, hd), lambda b, gg, qi: (slot, gg, b, 0, 0))
    return pl.pallas_call(
        functools.partial(_nsa_kernel, tq=tq, tc=tc, hg=hg, hd=hd, max_chunks=t_len // tc),
        grid=(bsz, g, nq),
        in_specs=[
            pl.BlockSpec((tq, hg * hd), lambda b, gg, qi: (b * nq + qi, gg)),
            pl.BlockSpec((tq, LANES), lambda b, gg, qi: (b * nq + qi, 0)),
            cmp_spec(0), cmp_spec(1),
            kv_spec(0), kv_spec(1), kv_spec(2), kv_spec(3),
            pl.BlockSpec((n_slc, hg * ncmp_pad), lambda b, gg, qi: (0, 0)),
            pl.BlockSpec((n_slc, t_len), lambda b, gg, qi: (0, 0)),
        ],
        out_specs=pl.BlockSpec((tq, hg * hd), lambda b, gg, qi: (b * nq + qi, gg)),
        out_shape=jax.ShapeDtypeStruct((bsz * t_len, g * hg * hd), BF16),
        scratch_shapes=[pltpu.VMEM((tq, t_len), F32)],
        compiler_params=_params("parallel", "parallel", "arbitrary"),
        name="nsa_attention",
    )(q, gates, kv_cmp, kv_cmp, kv_rest, kv_rest, kv_rest, kv_rest, ovt, expand)


def _rope_tables(pos, rot_dims, theta):
    half = rot_dims // 2
    inv_freq = jnp.power(jnp.float32(theta), -jnp.arange(half, dtype=F32) / half)
    ang = pos.astype(F32)[:, None] * inv_freq[None, :]
    cos, sin = jnp.cos(ang), jnp.sin(ang)
    n = pos.shape[0]
    pad = LANES - rot_dims
    c_tab = jnp.concatenate([cos, cos, jnp.ones((n, pad), F32)], axis=1)
    s_tab = jnp.concatenate([-sin, sin, jnp.zeros((n, pad), F32)], axis=1)
    return c_tab, s_tab


def kernel(x, ln_g, ln_b, ffn1_w_in, ffn1_w_out, ffn2_w_in, ffn2_w_out, ret_w_in, ret_gn_g, ret_gn_b, ret_w_out, kv_w, cmp_pos, cmp_w1, cmp_b1, cmp_w2, nsa_w_q, nsa_w_out):
    bsz, t_len, d = x.shape
    m = bsz * t_len
    bf = lambda w: w.astype(BF16)
    h = x.reshape(m, d)
    pos = jnp.arange(t_len)

    h, hb, w_in_next, w_out_next = _ffn(h, bf(ffn1_w_in[0]), bf(ffn1_w_out[0]), ln_g[0, 0], ln_b[0, 0],
                                        cast_next=(ffn2_w_in, ffn2_w_out, 0), emit_bf16=True)
    dk = d // RET_HEADS
    dv = 2 * d // RET_HEADS
    half = dk // 2
    inv_freq = jnp.power(jnp.float32(RET_ROPE_BASE), -jnp.arange(half, dtype=F32) / half)
    ang = pos.astype(F32)[:, None] * inv_freq[None, :]
    n_qk, n_vg = 2 * RET_HEADS * dk, 2 * RET_HEADS * dv
    tile_spec = lambda tm, tn: pl.BlockSpec((tm, tn), lambda j, i: (i, j))
    rot_epi = functools.partial(_qk_rot_epilogue, q_tiles=RET_HEADS * dk // PROJ_COL_TILE, k_scale=dk ** -0.5,
                                half=half)
    qk = _wproj(hb, ret_w_in, col0=0, ncols=n_qk, epilogue=rot_epi, out_spec=tile_spec,
                out_shape=jax.ShapeDtypeStruct((m, n_qk), BF16),
                extra=(jnp.cos(ang), jnp.sin(ang)), extra_specs=_table_specs(t_len, half), name="ret_qk_proj")
    vg = _wproj(hb, ret_w_in, col0=n_qk, ncols=n_vg, epilogue=_store_epilogue, out_spec=tile_spec,
                out_shape=jax.ShapeDtypeStruct((m, n_vg), BF16), name="ret_vg_proj")
    y = _retention(qk, vg, ret_gn_g[0], ret_gn_b[0], bsz, t_len, dk, dv)
    h = _out_proj(y, bf(ret_w_out), h, ln_g[0, 1], ln_b[0, 1], "ret_out_proj")
    h, hb, w_in_next, w_out_next = _ffn(h, w_in_next, w_out_next, ln_g[0, 2], ln_b[0, 2],
                                        cast_next=(ffn1_w_in, ffn1_w_out, 1), emit_bf16=True)

    hd = d // NSA_HEADS
    rope_dims = hd // 4
    rope_half = rope_dims // 2
    c_tab, s_tab = _rope_tables(pos, rope_dims, ROPE_THETA)
    w_kv = kv_w[None]
    slot_cols = NSA_KV_GROUPS * hd
    raw_cmp = _head_proj(hb, w_kv, c_tab, s_tab, t_len, col0=0, nslots=2, out_dtype=F32, half=rope_half,
                         scale=1.0, rotate="none", split=True, name="kv_cmp_proj")
    kv_rest = _head_proj(hb, w_kv, c_tab, s_tab, t_len, col0=2 * slot_cols, nslots=4, out_dtype=BF16,
                         half=rope_half, scale=1.0, rotate="even", split=True, name="kv_slc_win_proj")
    n_rows = t_len // CMP_STRIDE
    cmp_end = jnp.arange(n_rows) * CMP_STRIDE + CMP_BLOCK - 1
    cc_tab, cs_tab = _rope_tables(cmp_end, rope_dims, ROPE_THETA)
    kv_cmp = _cmp_mlp(raw_cmp, cmp_pos, bf(cmp_w1), cmp_b1, bf(cmp_w2), cc_tab, cs_tab, bsz, t_len, rope_half)

    h, hb, w_in_next, w_out_next = _ffn(h, w_in_next, w_out_next, ln_g[1, 0], ln_b[1, 0],
                                        cast_next=(ffn2_w_in, ffn2_w_out, 1), emit_bf16=True)
    n_q = NSA_HEADS * hd
    q = _head_proj(hb, nsa_w_q, c_tab, s_tab, t_len, col0=0, nslots=n_q // slot_cols, out_dtype=BF16,
                   half=rope_half, scale=hd ** -0.5 * LOG2_E, rotate="all", split=False, name="nsa_q_proj")
    n_gate = NSA_HEADS * 3
    w_gate = jnp.pad(nsa_w_q[0][:, n_q:], ((0, 0), (0, LANES - n_gate)))[None]
    gates = _wproj(hb, w_gate, col0=0, ncols=LANES, epilogue=_sigmoid_epilogue,
                   out_spec=lambda tm, tn: pl.BlockSpec((tm, tn), lambda j, i: (i, j)),
                   out_shape=jax.ShapeDtypeStruct((m, LANES), F32), name="nsa_gate_proj")
    o = _nsa_attention(q, gates, kv_cmp, kv_rest, bsz, t_len)
    h = _out_proj(o, bf(nsa_w_out), h, ln_g[1, 1], ln_b[1, 1], "nsa_out_proj")
    (h,) = _ffn(h, w_in_next, w_out_next, ln_g[1, 2], ln_b[1, 2])
    return h.reshape(bsz, t_len, d)
```

```python
import functools
import math

import numpy as np
import jax
import jax.numpy as jnp
from jax import lax
from jax.experimental import pallas as pl
from jax.experimental.pallas import tpu as pltpu

F32 = jnp.float32
BF16 = jnp.bfloat16

DEPTH = 2
DEEPNORM_ALPHA = (2 * DEPTH) ** 0.25
LN_EPS = 1e-5
MACARON_WEIGHT = 0.5

RET_HEADS = 8
RET_CHUNK = 256
RET_ROPE_BASE = 10000.0

NSA_HEADS = 16
NSA_KV_GROUPS = 4
NSA_HEADS_PER_GROUP = NSA_HEADS // NSA_KV_GROUPS
CMP_BLOCK = 32
CMP_STRIDE = 16
SLC_BLOCK = 64
SLC_TOP = 8
WINDOW = 512
FORCE_BONUS = 1e4
ROPE_THETA = 500000.0
NEG_INF = -1e30
LOG2_E = math.log2(math.e)

V7X_VMEM_BYTES = 64 * 1024 * 1024
VMEM_LIMIT = V7X_VMEM_BYTES - 8 * 1024 * 1024
LANES = 128

FFN_ROW_TILE = 512
FFN_HIDDEN_TILE = 512
FFN_SUBTILES = 2
PROJ_ROW_TILE = 1024
PROJ_COL_TILE = 1024
OUT_PROJ_ROW_TILE = 512
NSA_Q_TILE = 256
NSA_KEY_CHUNK = 256


def _params(*semantics):
    return pltpu.CompilerParams(dimension_semantics=semantics, vmem_limit_bytes=VMEM_LIMIT)


def _layer_norm(y, g, b, eps=LN_EPS):
    mu = jnp.mean(y, axis=-1, keepdims=True)
    d = y - mu
    var = jnp.mean(d * d, axis=-1, keepdims=True)
    return d * lax.rsqrt(var + eps) * g + b


def _dot(a, b):
    return jnp.dot(a, b, preferred_element_type=F32)


def _dot_nt(a, b):
    return lax.dot_general(a, b, (((1,), (1,)), ((), ())), preferred_element_type=F32)


def _dot_tn(a, b):
    return lax.dot_general(a, b, (((0,), (0,)), ((), ())), preferred_element_type=F32)


def _ffn_kernel(*refs, nf, cast_next, emit_bf16):
    refs = list(refs)
    x_ref, wa_ref, wu_ref, wo_ref, g_ref, b_ref = refs[:6]
    refs = refs[6:]
    if cast_next:
        nwi_ref, nwo_ref = refs[:2]
        refs = refs[2:]
    o_ref = refs.pop(0)
    ob_ref = refs.pop(0) if emit_bf16 else None
    xb_ref = refs.pop()
    if cast_next:
        nwi_out, nwo_out = refs
        nwi_out[...] = nwi_ref[...].astype(BF16)
        nwo_out[...] = nwo_ref[...].astype(BF16)
    f = pl.program_id(1)

    @pl.when(f == 0)
    def _():
        x = x_ref[...]
        xb_ref[...] = x.astype(BF16)
        o_ref[...] = (DEEPNORM_ALPHA / MACARON_WEIGHT) * x

    xb = xb_ref[...]
    sub = wa_ref.shape[1] // FFN_SUBTILES
    part = None
    for s in range(FFN_SUBTILES):
        a = _dot(xb, wa_ref[:, s * sub:(s + 1) * sub])
        u = _dot(xb, wu_ref[:, s * sub:(s + 1) * sub])
        h = (a * jax.nn.sigmoid(a)) * u
        p = _dot(h.astype(BF16), wo_ref[s * sub:(s + 1) * sub, :])
        part = p if part is None else part + p
    o_ref[...] += part

    @pl.when(f == nf - 1)
    def _():
        out = _layer_norm(o_ref[...], g_ref[...], b_ref[...], eps=LN_EPS / MACARON_WEIGHT ** 2)
        o_ref[...] = out
        if emit_bf16:
            ob_ref[...] = out.astype(BF16)


def _ffn(x, w_in, w_out, g, b, cast_next=None, emit_bf16=False):
    m, d = x.shape
    f_dim = w_out.shape[0]
    tm, tf = FFN_ROW_TILE, FFN_HIDDEN_TILE
    ni, nf = m // tm, f_dim // tf
    row_spec = pl.BlockSpec((tm, d), lambda i, f: (i, 0))
    vec_spec = pl.BlockSpec((1, d), lambda i, f: (0, 0))
    in_specs = [row_spec, pl.BlockSpec((d, tf), lambda i, f: (0, f)), pl.BlockSpec((d, tf), lambda i, f: (0, f + nf)),
                pl.BlockSpec((tf, d), lambda i, f: (f, 0)), vec_spec, vec_spec]
    args = [x, w_in, w_in, w_out, g.reshape(1, d), b.reshape(1, d)]
    out_specs, out_shape = [row_spec], [jax.ShapeDtypeStruct((m, d), F32)]
    if emit_bf16:
        out_specs.append(row_spec)
        out_shape.append(jax.ShapeDtypeStruct((m, d), BF16))
    if cast_next is not None:
        nw_in, nw_out, layer = cast_next
        ri, ci = d // ni, 2 * f_dim // nf
        ro = f_dim // (ni * nf)
        assert ri % 16 == 0 and ci % LANES == 0 and ro % 16 == 0
        in_specs += [pl.BlockSpec((None, ri, ci), lambda i, f: (layer, i, f)),
                     pl.BlockSpec((None, ro, d), lambda i, f: (layer, i * nf + f, 0))]
        out_specs += [pl.BlockSpec((ri, ci), lambda i, f: (i, f)), pl.BlockSpec((ro, d), lambda i, f: (i * nf + f, 0))]
        out_shape += [jax.ShapeDtypeStruct((d, 2 * f_dim), BF16), jax.ShapeDtypeStruct((f_dim, d), BF16)]
        args += [nw_in, nw_out]
    return pl.pallas_call(
        functools.partial(_ffn_kernel, nf=nf, cast_next=cast_next is not None, emit_bf16=emit_bf16),
        grid=(ni, nf),
        in_specs=in_specs,
        out_specs=out_specs,
        out_shape=out_shape,
        scratch_shapes=[pltpu.VMEM((tm, d), BF16)],
        compiler_params=_params("arbitrary", "arbitrary"),
        name="ffn_deepnorm",
    )(*args)


def _wproj_kernel(*refs, n_extra, epilogue):
    xb_ref, w_ref = refs[:2]
    extra = refs[2:2 + n_extra]
    o_ref, wb_ref = refs[2 + n_extra:]

    @pl.when(pl.program_id(1) == 0)
    def _():
        wb_ref[...] = w_ref[...].astype(BF16)

    epilogue(_dot(xb_ref[...], wb_ref[...]), pl.program_id(0), extra, o_ref)


def _wproj(xb, w, *, col0, ncols, epilogue, out_spec, out_shape, extra=(), extra_specs=(), name):
    m, k = xb.shape
    tm, tn = PROJ_ROW_TILE, min(PROJ_COL_TILE, ncols)
    assert col0 % tn == 0 and ncols % tn == 0
    return pl.pallas_call(
        functools.partial(_wproj_kernel, n_extra=len(extra), epilogue=epilogue),
        grid=(ncols // tn, m // tm),
        in_specs=[pl.BlockSpec((tm, k), lambda j, i: (i, 0)),
                  pl.BlockSpec((None, k, tn), lambda j, i: (0, 0, j + col0 // tn)), *extra_specs],
        out_specs=out_spec(tm, tn),
        out_shape=out_shape,
        scratch_shapes=[pltpu.VMEM((k, tn), BF16)],
        compiler_params=_params("arbitrary", "arbitrary"),
        name=name,
    )(xb, w, *extra)


def _table_specs(t_len, width, tm=PROJ_ROW_TILE):
    tpb = t_len // tm
    return [pl.BlockSpec((tm, width), lambda j, i: (i % tpb, 0))] * 2


def _qk_rot_epilogue(acc, j, extra, o_ref, *, q_tiles, k_scale, half):
    cos_ref, sin_ref = extra
    scale = jnp.where(j >= q_tiles, k_scale, 1.0).astype(F32)
    cos = cos_ref[...]
    sin = sin_ref[...]
    for h in range(acc.shape[1] // (2 * half)):
        c0 = h * 2 * half
        x1 = acc[:, c0:c0 + half]
        x2 = acc[:, c0 + half:c0 + 2 * half]
        o_ref[:, c0:c0 + half] = ((x1 * cos - x2 * sin) * scale).astype(o_ref.dtype)
        o_ref[:, c0 + half:c0 + 2 * half] = ((x2 * cos + x1 * sin) * scale).astype(o_ref.dtype)


def _store_epilogue(acc, j, extra, o_ref):
    o_ref[...] = acc.astype(o_ref.dtype)


def _partial_rotary(x, c, s, half):
    lane = lax.broadcasted_iota(jnp.int32, x.shape, 1)
    partner = jnp.where(lane < half, pltpu.roll(x, LANES - half, 1), pltpu.roll(x, half, 1))
    return x * c + partner * s


def _heads_epilogue(acc, j, extra, o_ref, *, hd, hps, half, scale, rotate, split):
    c_ref, s_ref = extra
    for h in range(acc.shape[1] // hd):
        piece = acc[:, h * hd:(h + 1) * hd]
        if rotate == "all" or (rotate == "even" and h < hps):
            piece = _partial_rotary(piece, c_ref[...], s_ref[...], half)
        if scale != 1.0:
            piece = piece * scale
        if split:
            o_ref[h // hps, h % hps] = piece.astype(o_ref.dtype)
        else:
            o_ref[:, h * hd:(h + 1) * hd] = piece.astype(o_ref.dtype)


def _sigmoid_epilogue(acc, j, extra, o_ref):
    o_ref[...] = jax.nn.sigmoid(acc).astype(o_ref.dtype)


def _head_proj(xb, w, c_tab, s_tab, t_len, *, col0, nslots, out_dtype, half, scale, rotate, split, name):
    m = xb.shape[0]
    hd, hps = LANES, NSA_KV_GROUPS
    assert nslots % 2 == 0
    if split:
        out_spec = lambda tm, tn: pl.BlockSpec((2, hps, tm, hd), lambda j, i: (j, 0, i, 0))
        out_shape = jax.ShapeDtypeStruct((nslots, hps, m, hd), out_dtype)
    else:
        out_spec = lambda tm, tn: pl.BlockSpec((tm, tn), lambda j, i: (i, j))
        out_shape = jax.ShapeDtypeStruct((m, nslots * hps * hd), out_dtype)
    epi = functools.partial(_heads_epilogue, hd=hd, hps=hps, half=half, scale=scale, rotate=rotate, split=split)
    return _wproj(xb, w, col0=col0, ncols=nslots * hps * hd, epilogue=epi, out_spec=out_spec, out_shape=out_shape,
                  extra=(c_tab, s_tab), extra_specs=_table_specs(t_len, hd), name=name)


def _out_proj_kernel(y_ref, w_ref, x_ref, g_ref, b_ref, o_ref, *, nsub):
    rs = y_ref.shape[0] // nsub
    for s in range(nsub):
        r = slice(s * rs, (s + 1) * rs)
        acc = _dot(y_ref[r, :], w_ref[...])
        o_ref[r, :] = _layer_norm(DEEPNORM_ALPHA * x_ref[r, :] + acc, g_ref[...], b_ref[...])


def _out_proj(y, w, x, g, b, name):
    m, k = y.shape
    d = w.shape[2]
    tm = OUT_PROJ_ROW_TILE
    row_spec = pl.BlockSpec((tm, d), lambda i: (i, 0))
    return pl.pallas_call(
        functools.partial(_out_proj_kernel, nsub=2),
        grid=(m // tm,),
        in_specs=[
            pl.BlockSpec((tm, k), lambda i: (i, 0)),
            pl.BlockSpec((None, k, d), lambda i: (0, 0, 0), pipeline_mode=pl.Buffered(1)),
            row_spec,
            pl.BlockSpec((1, d), lambda i: (0, 0)),
            pl.BlockSpec((1, d), lambda i: (0, 0)),
        ],
        out_specs=row_spec,
        out_shape=jax.ShapeDtypeStruct((m, d), F32),
        compiler_params=_params("parallel"),
        name=name,
    )(y, w, x, g.reshape(1, d), b.reshape(1, d))


def _retention_kernel(gam_ref, q_ref, k_ref, v_ref, g_ref, dec_ref, xi_ref, zeta_ref, gng_ref, gnb_ref,
                      o_ref, state_ref, *, nh, dk, dv):
    @pl.when(pl.program_id(1) == 0)
    def _():
        state_ref[...] = jnp.zeros_like(state_ref)

    for h in range(nh):
        q = q_ref[:, h * dk:(h + 1) * dk]
        k = k_ref[:, h * dk:(h + 1) * dk]
        v = v_ref[:, h * dv:(h + 1) * dv]
        state = state_ref[h]
        s = _dot_nt(q, k) * dec_ref[h]
        out = _dot(s.astype(BF16), v) + _dot(q, state.astype(BF16)) * xi_ref[h]
        kz = (k.astype(F32) * zeta_ref[h]).astype(BF16)
        state_ref[h] = _dot_tn(kz, v) + gam_ref[h] * state

        mu = jnp.mean(out, axis=-1, keepdims=True)
        d = out - mu
        var = jnp.mean(d * d, axis=-1, keepdims=True)
        cols = slice(h * dv, (h + 1) * dv)
        gn = d * lax.rsqrt(var + LN_EPS) * gng_ref[:, cols] + gnb_ref[:, cols]
        gate = g_ref[:, cols].astype(F32)
        o_ref[:, cols] = ((gate * jax.nn.sigmoid(gate)) * gn).astype(o_ref.dtype)


def _retention(qk, vg, gn_g, gn_b, bsz, t_len, dk, dv):
    h = RET_HEADS
    c = RET_CHUNK
    nc = t_len // c
    m = bsz * t_len
    log_gamma = jnp.log(1.0 - jnp.power(2.0, -5.0 - jnp.arange(h, dtype=F32)))
    idx = jnp.arange(c, dtype=F32)
    diff = idx[:, None] - idx[None, :]
    decay = jnp.where(diff >= 0, jnp.exp(log_gamma[:, None, None] * jnp.maximum(diff, 0.0)), 0.0)
    xi = jnp.exp(log_gamma[:, None] * (idx + 1.0))[..., None]
    zeta = jnp.exp(log_gamma[:, None] * (c - 1.0 - idx))[..., None]
    gamma_c = jnp.exp(log_gamma * c)
    row = lambda b, cc: b * nc + cc
    whole = lambda shape: pl.BlockSpec(shape, lambda b, cc: (0,) * len(shape))
    return pl.pallas_call(
        functools.partial(_retention_kernel, nh=h, dk=dk, dv=dv),
        grid=(bsz, nc),
        in_specs=[
            pl.BlockSpec(memory_space=pltpu.SMEM),
            pl.BlockSpec((c, h * dk), lambda b, cc: (row(b, cc), 0)),
            pl.BlockSpec((c, h * dk), lambda b, cc: (row(b, cc), 1)),
            pl.BlockSpec((c, h * dv), lambda b, cc: (row(b, cc), 0)),
            pl.BlockSpec((c, h * dv), lambda b, cc: (row(b, cc), 1)),
            whole((h, c, c)), whole((h, c, 1)), whole((h, c, 1)),
            whole((1, h * dv)), whole((1, h * dv)),
        ],
        out_specs=pl.BlockSpec((c, h * dv), lambda b, cc: (row(b, cc), 0)),
        out_shape=jax.ShapeDtypeStruct((m, h * dv), BF16),
        scratch_shapes=[pltpu.VMEM((h, dk, dv), F32)],
        compiler_params=_params("parallel", "arbitrary"),
        name="retention_chunks",
    )(gamma_c, qk, qk, vg, vg, decay, xi, zeta, gn_g.reshape(1, h * dv), gn_b.reshape(1, h * dv))


def _cmp_mlp_kernel(x_ref, pos_ref, w1a_ref, w1b_ref, b1_ref, w2_ref, c_ref, s_ref, o_ref, *, half):
    nrows = o_ref.shape[0]
    x = jnp.concatenate([x_ref[pl.ds(l, nrows, stride=CMP_STRIDE), :] for l in range(CMP_STRIDE)], axis=1)
    pos = pos_ref[...]
    a = _dot((x + pos[0:1, :]).astype(BF16), w1a_ref[...])
    bm = _dot((x + pos[1:2, :]).astype(BF16), w1b_ref[...])
    hid = jax.nn.gelu(a + pltpu.roll(bm, nrows - 1, 0) + b1_ref[...])
    comp = _dot(hid.astype(BF16), w2_ref[...])
    is_key = pl.program_id(0) == 0
    c = jnp.where(is_key, c_ref[...], 1.0)
    s = jnp.where(is_key, s_ref[...], 0.0)
    o_ref[...] = _partial_rotary(comp, c, s, half).astype(o_ref.dtype)


def _cmp_mlp(raw, cmp_pos, w1, b1, w2, c_tab, s_tab, bsz, t_len, half):
    g, hd = NSA_KV_GROUPS, LANES
    nrows = t_len // CMP_STRIDE
    feat = CMP_STRIDE * hd
    hidden = w1.shape[-1]
    pos = cmp_pos.reshape(2, CMP_BLOCK // CMP_STRIDE, feat)
    return pl.pallas_call(
        functools.partial(_cmp_mlp_kernel, half=half),
        grid=(2, g, bsz),
        in_specs=[
            pl.BlockSpec((None, None, t_len, hd), lambda c, gg, b: (c, gg, b, 0)),
            pl.BlockSpec((None, 2, feat), lambda c, gg, b: (c, 0, 0)),
            pl.BlockSpec((None, feat, hidden), lambda c, gg, b: (c, 0, 0)),
            pl.BlockSpec((None, feat, hidden), lambda c, gg, b: (c, 1, 0)),
            pl.BlockSpec((None, 1, hidden), lambda c, gg, b: (c, 0, 0)),
            pl.BlockSpec((None, hidden, hd), lambda c, gg, b: (c, 0, 0)),
            pl.BlockSpec((nrows, hd), lambda c, gg, b: (0, 0)),
            pl.BlockSpec((nrows, hd), lambda c, gg, b: (0, 0)),
        ],
        out_specs=pl.BlockSpec((None, None, None, nrows, hd), lambda c, gg, b: (c, gg, b, 0, 0)),
        out_shape=jax.ShapeDtypeStruct((2, g, bsz, nrows, hd), BF16),
        compiler_params=_params("arbitrary", "arbitrary", "arbitrary"),
        name="cmp_mlp",
    )(raw, pos, w1, w1, b1.reshape(2, 1, hidden), w2, c_tab, s_tab)


def _with_ones(v):
    return jnp.concatenate([v, jnp.ones_like(v)], axis=1)


def _nsa_kernel(q_ref, gate_ref, kc_ref, vc_ref, ks_ref, vs_ref, kw_ref, vw_ref, ovt_ref, ex_ref,
                o_ref, mask_ref, *, tq, tc, hg, hd, max_chunks):
    qi = pl.program_id(2)
    t0 = qi * tq
    rows = hg * tq
    q = q_ref[...]
    q4 = jnp.concatenate([q[:, i * hd:(i + 1) * hd] for i in range(hg)], axis=0)
    tpos = t0 + lax.broadcasted_iota(jnp.int32, (tq, 1), 0)

    ncmp = kc_ref.shape[0]
    s3 = _dot_nt(q4, kc_ref[...]).reshape(hg, tq, ncmp)
    cmp_end = lax.broadcasted_iota(jnp.int32, (tq, ncmp), 1) * CMP_STRIDE + (CMP_BLOCK - 1)
    s3 = jnp.where((cmp_end <= tpos)[None], s3, NEG_INF)
    e3 = jnp.exp2(s3 - jnp.max(s3, axis=-1, keepdims=True))
    has_cmp = (tpos >= CMP_BLOCK - 1).astype(F32)
    p3 = (e3 / jnp.sum(e3, axis=-1, keepdims=True) * has_cmp[None]).astype(BF16)
    o_cmp = _dot(p3.reshape(rows, ncmp), vc_ref[...])

    nslc = ovt_ref.shape[0]
    p_slc = _dot_nt(ovt_ref[...], jnp.concatenate([p3[i] for i in range(hg)], axis=1))
    trow = t0 + lax.broadcasted_iota(jnp.int32, (1, tq), 1)
    blk = lax.broadcasted_iota(jnp.int32, (nslc, tq), 0)
    cur = jnp.right_shift(trow, SLC_BLOCK.bit_length() - 1)
    valid = blk * SLC_BLOCK <= trow
    forced = (blk == 0) | (blk == cur) | (blk == cur - 1)
    score = jnp.where(valid, p_slc + jnp.where(forced, FORCE_BONUS, 0.0), -1.0)
    rank = jnp.zeros((nslc, tq), jnp.int32)
    for kb in range(nslc):
        ck = score[kb:kb + 1, :]
        rank += ((ck > score) | ((ck == score) & (blk > kb))).astype(jnp.int32)
    sel = jnp.where(rank < SLC_TOP, 1.0, 0.0).astype(BF16)
    mask_ref[...] = _dot_tn(sel, ex_ref[...])

    gts = gate_ref[...]
    lane = lax.broadcasted_iota(jnp.int32, (1, gts.shape[1]), 1)
    gate0 = pl.program_id(1) * (3 * hg)
    gate_col = [jnp.sum(jnp.where(lane == gate0 + c, gts, 0.0), axis=1, keepdims=True) for c in range(3 * hg)]

    def window_and_selected(n_chunks, last_len):
        wlen = min(WINDOW + tq, (n_chunks - 1) * tc + last_len)
        w0 = pl.multiple_of(jnp.maximum(qi - WINDOW // tq, 0) * tq, tq)
        sw = _dot_nt(q4, kw_ref[pl.ds(w0, wlen), :]).reshape(hg, tq, wlen)
        kpos = w0 + lax.broadcasted_iota(jnp.int32, (tq, wlen), 1)
        okw = (kpos <= tpos) & (kpos > tpos - WINDOW)
        sw = jnp.where(okw[None], sw, NEG_INF)
        ew = jnp.exp2(sw - jnp.max(sw, axis=-1, keepdims=True)).astype(BF16)
        rw = _dot(ew.reshape(rows, wlen), _with_ones(vw_ref[pl.ds(w0, wlen), :]))
        o_win = rw[:, :hd] / rw[:, hd:]

        m_i = acc = None
        for c in range(n_chunks):
            k0 = c * tc
            kl = last_len if c == n_chunks - 1 else tc
            s = _dot_nt(q4, ks_ref[k0:k0 + kl, :]).reshape(hg, tq, kl)
            ok = mask_ref[:, k0:k0 + kl] > 0.5
            if c == n_chunks - 1:
                ok &= (k0 + lax.broadcasted_iota(jnp.int32, (tq, kl), 1)) <= tpos
            s = jnp.where(ok[None], s, NEG_INF)
            m_c = jnp.max(s, axis=-1, keepdims=True)
            m_new = m_c if c == 0 else jnp.maximum(m_i, m_c)
            pv = _dot(jnp.exp2(s - m_new).astype(BF16).reshape(rows, kl), _with_ones(vs_ref[k0:k0 + kl, :]))
            acc = pv if c == 0 else acc * jnp.exp2(m_i - m_new).reshape(rows, 1) + pv
            m_i = m_new
        o_slc = acc[:, :hd] / acc[:, hd:]

        for i in range(hg):
            r = slice(i * tq, (i + 1) * tq)
            o = gate_col[3 * i] * o_cmp[r] + gate_col[3 * i + 1] * o_slc[r] + gate_col[3 * i + 2] * o_win[r]
            o_ref[:, i * hd:(i + 1) * hd] = o.astype(o_ref.dtype)

    n_chunks = lax.div(t0, tc) + 1
    offset = lax.rem(t0, tc)
    for n in range(1, max_chunks + 1):
        for off in range(0, tc, tq):
            pl.when((n_chunks == n) & (offset == off))(functools.partial(window_and_selected, n, off + tq))


def _nsa_attention(q, gates, kv_cmp, kv_rest, bsz, t_len):
    g, hg, hd = NSA_KV_GROUPS, NSA_HEADS_PER_GROUP, LANES
    tq, tc = NSA_Q_TILE, NSA_KEY_CHUNK
    nq = t_len // tq
    ncmp_pad = t_len // CMP_STRIDE
    n_cmp = (t_len - CMP_BLOCK) // CMP_STRIDE + 1
    n_slc = t_len // SLC_BLOCK
    cs = np.arange(ncmp_pad)[None, :] * CMP_STRIDE
    ss = np.arange(n_slc)[:, None] * SLC_BLOCK
    ovt = ((cs < ss + SLC_BLOCK) & (cs + CMP_BLOCK > ss) & (np.arange(ncmp_pad)[None, :] < n_cmp)).astype(np.float32)
    ovt = jnp.asarray(np.tile(ovt, (1, hg)), BF16)
    expand = jnp.asarray((np.arange(t_len)[None, :] // SLC_BLOCK == np.arange(n_slc)[:, None]).astype(np.float32), BF16)
    kv_spec = lambda slot: pl.BlockSpec((None, None, t_len, hd), lambda b, gg, qi: (slot, gg, b, 0))
    cmp_spec = lambda slot: pl.BlockSpec((None, None, None, ncmp_pad, hd), lambda b, gg, qi: (slot, gg, b, 0, 0))
    return pl.pallas_call(
        functools.partial(_nsa_kernel, tq=tq, tc=tc, hg=hg, hd=hd, max_chunks=t_len // tc),
        grid=(bsz, g, nq),
        in_specs=[
            pl.BlockSpec((tq, hg * hd), lambda b, gg, qi: (b * nq + qi, gg)),
            pl.BlockSpec((tq, LANES), lambda b, gg, qi: (b * nq + qi, 0)),
            cmp_spec(0), cmp_spec(1),
            kv_spec(0), kv_spec(1), kv_spec(2), kv_spec(3),
            pl.BlockSpec((n_slc, hg * ncmp_pad), lambda b, gg, qi: (0, 0)),
            pl.BlockSpec((n_slc, t_len), lambda b, gg, qi: (0, 0)),
        ],
        out_specs=pl.BlockSpec((tq, hg * hd), lambda b, gg, qi: (b * nq + qi, gg)),
        out_shape=jax.ShapeDtypeStruct((bsz * t_len, g * hg * hd), BF16),
        scratch_shapes=[pltpu.VMEM((tq, t_len), F32)],
        compiler_params=_params("parallel", "parallel", "arbitrary"),
        name="nsa_attention",
    )(q, gates, kv_cmp, kv_cmp, kv_rest, kv_rest, kv_rest, kv_rest, ovt, expand)


def _rope_tables(pos, rot_dims, theta):
    half = rot_dims // 2
    inv_freq = jnp.power(jnp.float32(theta), -jnp.arange(half, dtype=F32) / half)
    ang = pos.astype(F32)[:, None] * inv_freq[None, :]
    cos, sin = jnp.cos(ang), jnp.sin(ang)
    n = pos.shape[0]
    pad = LANES - rot_dims
    c_tab = jnp.concatenate([cos, cos, jnp.ones((n, pad), F32)], axis=1)
    s_tab = jnp.concatenate([-sin, sin, jnp.zeros((n, pad), F32)], axis=1)
    return c_tab, s_tab


def kernel(x, ln_g, ln_b, ffn1_w_in, ffn1_w_out, ffn2_w_in, ffn2_w_out, ret_w_in, ret_gn_g, ret_gn_b, ret_w_out, kv_w, cmp_pos, cmp_w1, cmp_b1, cmp_w2, nsa_w_q, nsa_w_out):
    bsz, t_len, d = x.shape
    m = bsz * t_len
    bf = lambda w: w.astype(BF16)
    h = x.reshape(m, d)
    pos = jnp.arange(t_len)

    h, hb, w_in_next, w_out_next = _ffn(h, bf(ffn1_w_in[0]), bf(ffn1_w_out[0]), ln_g[0, 0], ln_b[0, 0],
                                        cast_next=(ffn2_w_in, ffn2_w_out, 0), emit_bf16=True)
    dk = d // RET_HEADS
    dv = 2 * d // RET_HEADS
    half = dk // 2
    inv_freq = jnp.power(jnp.float32(RET_ROPE_BASE), -jnp.arange(half, dtype=F32) / half)
    ang = pos.astype(F32)[:, None] * inv_freq[None, :]
    n_qk, n_vg = 2 * RET_HEADS * dk, 2 * RET_HEADS * dv
    tile_spec = lambda tm, tn: pl.BlockSpec((tm, tn), lambda j, i: (i, j))
    rot_epi = functools.partial(_qk_rot_epilogue, q_tiles=RET_HEADS * dk // PROJ_COL_TILE, k_scale=dk ** -0.5,
                                half=half)
    qk = _wproj(hb, ret_w_in, col0=0, ncols=n_qk, epilogue=rot_epi, out_spec=tile_spec,
                out_shape=jax.ShapeDtypeStruct((m, n_qk), BF16),
                extra=(jnp.cos(ang), jnp.sin(ang)), extra_specs=_table_specs(t_len, half), name="ret_qk_proj")
    vg = _wproj(hb, ret_w_in, col0=n_qk, ncols=n_vg, epilogue=_store_epilogue, out_spec=tile_spec,
                out_shape=jax.ShapeDtypeStruct((m, n_vg), BF16), name="ret_vg_proj")
    y = _retention(qk, vg, ret_gn_g[0], ret_gn_b[0], bsz, t_len, dk, dv)
    h = _out_proj(y, bf(ret_w_out), h, ln_g[0, 1], ln_b[0, 1], "ret_out_proj")
    h, hb, w_in_next, w_out_next = _ffn(h, w_in_next, w_out_next, ln_g[0, 2], ln_b[0, 2],
                                        cast_next=(ffn1_w_in, ffn1_w_out, 1), emit_bf16=True)

    hd = d // NSA_HEADS
    rope_dims = hd // 4
    rope_half = rope_dims // 2
    c_tab, s_tab = _rope_tables(pos, rope_dims, ROPE_THETA)
    w_kv = kv_w[None]
    slot_cols = NSA_KV_GROUPS * hd
    raw_cmp = _head_proj(hb, w_kv, c_tab, s_tab, t_len, col0=0, nslots=2, out_dtype=F32, half=rope_half,
                         scale=1.0, rotate="none", split=True, name="kv_cmp_proj")
    kv_rest = _head_proj(hb, w_kv, c_tab, s_tab, t_len, col0=2 * slot_cols, nslots=4, out_dtype=BF16,
                         half=rope_half, scale=1.0, rotate="even", split=True, name="kv_slc_win_proj")
    n_rows = t_len // CMP_STRIDE
    cmp_end = jnp.arange(n_rows) * CMP_STRIDE + CMP_BLOCK - 1
    cc_tab, cs_tab = _rope_tables(cmp_end, rope_dims, ROPE_THETA)
    kv_cmp = _cmp_mlp(raw_cmp, cmp_pos, bf(cmp_w1), cmp_b1, bf(cmp_w2), cc_tab, cs_tab, bsz, t_len, rope_half)

    h, hb, w_in_next, w_out_next = _ffn(h, w_in_next, w_out_next, ln_g[1, 0], ln_b[1, 0],
                                        cast_next=(ffn2_w_in, ffn2_w_out, 1), emit_bf16=True)
    n_q = NSA_HEADS * hd
    q = _head_proj(hb, nsa_w_q, c_tab, s_tab, t_len, col0=0, nslots=n_q // slot_cols, out_dtype=BF16,
                   half=rope_half, scale=hd ** -0.5 * LOG2_E, rotate="all", split=False, name="nsa_q_proj")
    n_gate = NSA_HEADS * 3
    w_gate = jnp.pad(nsa_w_q[0][:, n_q:], ((0, 0), (0, LANES - n_gate)))[None]
    gates = _wproj(hb, w_gate, col0=0, ncols=LANES, epilogue=_sigmoid_epilogue,
                   out_spec=lambda tm, tn: pl.BlockSpec((tm, tn), lambda j, i: (i, j)),
                   out_shape=jax.ShapeDtypeStruct((m, LANES), F32), name="nsa_gate_proj")
    o = _nsa_attention(q, gates, kv_cmp, kv_rest, bsz, t_len)
    h = _out_proj(o, bf(nsa_w_out), h, ln_g[1, 1], ln_b[1, 1], "nsa_out_proj")
    (h,) = _ffn(h, w_in_next, w_out_next, ln_g[1, 2], ln_b[1, 2])
    return h.reshape(bsz, t_len, d)
```
